```python
import jax, jax.numpy as jnp
from jax import lax
import numpy as np

D_MODEL = 2048
BATCH = 8
SEQ = 2048
DEPTH = 4

CHUNK = 64
D_RNN = D_MODEL // 2
RG_HEADS = 16
RG_HEAD_DIM = D_RNN // RG_HEADS
RG_CONV = 4
RG_C = 8.0
D_CONV = D_MODEL // 2
SC_WIDTH = 3
D_FF = 3 * D_MODEL
N_EXPERTS = 8
TOP_K = 2
D_FF_EXPERT = 3 * D_MODEL // 2
N_MOD = 6
EPS = 1e-6
IN_SPLITS = (D_RNN, 2 * D_RNN, 2 * D_RNN + D_CONV, 2 * D_RNN + 2 * D_CONV,
             2 * D_RNN + 3 * D_CONV, 2 * D_RNN + 3 * D_CONV + D_MODEL)
D_IN = 2 * D_RNN + 3 * D_CONV + 2 * D_MODEL

kernel_name = "hybrid_rglru_shortconv_moe_adaln"


def _rmsnorm(x, g):
    xf = x.astype(jnp.float32)
    y = xf * lax.rsqrt(jnp.mean(xf * xf, axis=-1, keepdims=True) + EPS)
    return (y * g.astype(jnp.float32)).astype(x.dtype)


def _modulate(h, shift, scale):
    return h * (1 + scale[:, None, :]) + shift[:, None, :]


def _causal_dwconv(x, w, b):
    k = w.shape[0]
    s = x.shape[1]
    xp = jnp.pad(x, ((0, 0), (k - 1, 0), (0, 0)))
    y = b
    for j in range(k):
        y = y + xp[:, j:j + s, :] * w[j]
    return y


def _scan_combine(left, right):
    a_l, b_l = left
    a_r, b_r = right
    return a_l * a_r, a_r * b_l + b_r


def _rg_lru(xc, r, i, lam):
    bsz, s, w = xc.shape
    nc = s // CHUNK
    log_a = -RG_C * r.astype(jnp.float32) * jax.nn.softplus(-lam.astype(jnp.float32))
    a = jnp.exp(log_a)
    u = jnp.sqrt(-jnp.expm1(2.0 * log_a)) * (i.astype(jnp.float32) * xc.astype(jnp.float32))
    a = a.reshape(bsz, nc, CHUNK, w)
    u = u.reshape(bsz, nc, CHUNK, w)
    a_cum, h_loc = lax.associative_scan(_scan_combine, (a, u), axis=2)

    def carry_step(h_prev, chunk_end):
        a_end, h_end = chunk_end
        return a_end * h_prev + h_end, h_prev

    _, h_in = lax.scan(carry_step, jnp.zeros((bsz, w), jnp.float32),
                       (jnp.moveaxis(a_cum[:, :, -1], 1, 0), jnp.moveaxis(h_loc[:, :, -1], 1, 0)))
    h = h_loc + a_cum * jnp.moveaxis(h_in, 0, 1)[:, :, None, :]
    return h.reshape(bsz, s, w).astype(xc.dtype)


def _hybrid_mixer(h, w_in, conv_a_w, conv_a_b, w_rg_a, b_rg_a, w_rg_x, b_rg_x, rg_lambda,
                  w_out_a, conv_b_w, conv_b_b, w_out_b, w_o):
    z = h @ w_in
    a_x, a_g, b_v, b_bg, b_cg, g_a, g_b = jnp.split(z, IN_SPLITS, axis=-1)
    bsz, s, _ = h.shape
    xc = _causal_dwconv(a_x, conv_a_w, conv_a_b)
    xh = xc.reshape(bsz, s, RG_HEADS, RG_HEAD_DIM)
    r = jax.nn.sigmoid(jnp.einsum('bshd,hde->bshe', xh, w_rg_a).reshape(bsz, s, D_RNN) + b_rg_a)
    i = jax.nn.sigmoid(jnp.einsum('bshd,hde->bshe', xh, w_rg_x).reshape(bsz, s, D_RNN) + b_rg_x)
    hr = _rg_lru(xc, r, i, rg_lambda)
    y_a = (jax.nn.gelu(a_g, approximate=True) * hr) @ w_out_a
    y_b = (b_bg * _causal_dwconv(b_cg * b_v, conv_b_w, conv_b_b)) @ w_out_b
    merged = jax.nn.sigmoid(g_a) * y_a + jax.nn.sigmoid(g_b) * y_b
    return merged @ w_o


def _swiglu(h, wg, wu, wd):
    return (jax.nn.silu(h @ wg) * (h @ wu)) @ wd


def _moe(h, w_router, b_router, w_e_gate, w_e_up, w_e_down):
    bsz, s, d = h.shape
    t = h.reshape(bsz * s, d)
    logits = (t @ w_router).astype(jnp.float32) + b_router.astype(jnp.float32)
    top_vals, top_idx = lax.top_k(logits, TOP_K)
    top_w = jax.nn.softmax(top_vals, axis=-1)
    gates = jnp.einsum('nk,nke->ne', top_w,
                       jax.nn.one_hot(top_idx, N_EXPERTS, dtype=jnp.float32)).astype(h.dtype)
    out = jnp.zeros_like(t)
    for e in range(N_EXPERTS):
        out = out + gates[:, e:e + 1] * _swiglu(t, w_e_gate[e], w_e_up[e], w_e_down[e])
    return out.reshape(bsz, s, d)


def setup_inputs(seed: int = 0) -> dict:
    key = jax.random.key(seed)
    ks = jax.random.split(key, 32)
    f32 = jnp.float32
    n_dense = (DEPTH + 1) // 2
    n_moe = DEPTH // 2

    def nrm(k, shape, scale):
        return jax.random.normal(k, shape, f32) * scale

    a_c = jax.random.uniform(ks[13], (DEPTH, D_RNN), f32, minval=0.9, maxval=0.999)
    a0 = a_c ** (1.0 / RG_C)
    rg_lambda = jnp.log(a0) - jnp.log1p(-a0)
    return {
        "x": nrm(ks[0], (BATCH, SEQ, D_MODEL), 1.0),
        "c": nrm(ks[1], (BATCH, D_MODEL), 1.0),
        "w_mod": nrm(ks[2], (DEPTH, D_MODEL, N_MOD * D_MODEL), 0.5 * D_MODEL ** -0.5),
        "b_mod": nrm(ks[3], (DEPTH, N_MOD * D_MODEL), 0.02),
        "norm1_g": 1.0 + nrm(ks[4], (DEPTH, D_MODEL), 0.02),
        "norm2_g": 1.0 + nrm(ks[5], (DEPTH, D_MODEL), 0.02),
        "w_in": nrm(ks[6], (DEPTH, D_MODEL, D_IN), D_MODEL ** -0.5),
        "conv_a_w": nrm(ks[7], (DEPTH, RG_CONV, D_RNN), RG_CONV ** -0.5),
        "conv_a_b": nrm(ks[8], (DEPTH, D_RNN), 0.02),
        "w_rg_a": nrm(ks[9], (DEPTH, RG_HEADS, RG_HEAD_DIM, RG_HEAD_DIM), RG_HEAD_DIM ** -0.5),
        "b_rg_a": nrm(ks[10], (DEPTH, D_RNN), 0.02),
        "w_rg_x": nrm(ks[11], (DEPTH, RG_HEADS, RG_HEAD_DIM, RG_HEAD_DIM), RG_HEAD_DIM ** -0.5),
        "b_rg_x": nrm(ks[12], (DEPTH, D_RNN), 0.02),
        "rg_lambda": rg_lambda,
        "w_out_a": nrm(ks[14], (DEPTH, D_RNN, D_MODEL), D_RNN ** -0.5),
        "conv_b_w": nrm(ks[15], (DEPTH, SC_WIDTH, D_CONV), SC_WIDTH ** -0.5),
        "conv_b_b": nrm(ks[16], (DEPTH, D_CONV), 0.02),
        "w_out_b": nrm(ks[17], (DEPTH, D_CONV, D_MODEL), D_CONV ** -0.5),
        "w_o": nrm(ks[18], (DEPTH, D_MODEL, D_MODEL), D_MODEL ** -0.5),
        "w_ff_gate": nrm(ks[19], (n_dense, D_MODEL, D_FF), D_MODEL ** -0.5),
        "w_ff_up": nrm(ks[20], (n_dense, D_MODEL, D_FF), D_MODEL ** -0.5),
        "w_ff_down": nrm(ks[21], (n_dense, D_FF, D_MODEL), D_FF ** -0.5),
        "w_router": nrm(ks[22], (n_moe, D_MODEL, N_EXPERTS), D_MODEL ** -0.5),
        "b_router": nrm(ks[23], (n_moe, N_EXPERTS), 0.01),
        "w_e_gate": nrm(ks[24], (n_moe, N_EXPERTS, D_MODEL, D_FF_EXPERT), D_MODEL ** -0.5),
        "w_e_up": nrm(ks[25], (n_moe, N_EXPERTS, D_MODEL, D_FF_EXPERT), D_MODEL ** -0.5),
        "w_e_down": nrm(ks[26], (n_moe, N_EXPERTS, D_FF_EXPERT, D_MODEL), D_FF_EXPERT ** -0.5),
        "final_g": 1.0 + nrm(ks[27], (D_MODEL,), 0.02),
    }


def reference(x, c, w_mod, b_mod, norm1_g, norm2_g, w_in, conv_a_w, conv_a_b, w_rg_a, b_rg_a,
              w_rg_x, b_rg_x, rg_lambda, w_out_a, conv_b_w, conv_b_b, w_out_b, w_o,
              w_ff_gate, w_ff_up, w_ff_down, w_router, b_router, w_e_gate, w_e_up, w_e_down,
              final_g):
    c_act = jax.nn.silu(c)
    for l in range(DEPTH):
        mod = c_act @ w_mod[l] + b_mod[l]
        sh1, sc1, g1, sh2, sc2, g2 = jnp.split(mod, N_MOD, axis=-1)
        h = _modulate(_rmsnorm(x, norm1_g[l]), sh1, sc1)
        y = _hybrid_mixer(h, w_in[l], conv_a_w[l], conv_a_b[l], w_rg_a[l], b_rg_a[l],
                          w_rg_x[l], b_rg_x[l], rg_lambda[l], w_out_a[l], conv_b_w[l],
                          conv_b_b[l], w_out_b[l], w_o[l])
        x = x + g1[:, None, :] * y
        h = _modulate(_rmsnorm(x, norm2_g[l]), sh2, sc2)
        j = l // 2
        if l % 2 == 0:
            f = _swiglu(h, w_ff_gate[j], w_ff_up[j], w_ff_down[j])
        else:
            f = _moe(h, w_router[j], b_router[j], w_e_gate[j], w_e_up[j], w_e_down[j])
        x = x + g2[:, None, :] * f
    return _rmsnorm(x, final_g)
```

```python
import functools

import jax
import jax.numpy as jnp
from jax import lax
from jax.experimental import pallas as pl
from jax.experimental.pallas import tpu as pltpu

F32 = jnp.float32
BF16 = jnp.bfloat16

V7X_LANES = 128
V7X_SUBLANES = 8
V7X_MXU_DIM = 256
V7X_VMEM_BYTES = 64 * 1024 * 1024

RG_C = 8.0
RG_CONV = 4
SC_WIDTH = 3
N_MOD = 6
EPS = 1e-6
ROUTER_PAD_BIAS = -1e30


def _params(semantics, vmem_mib):
    assert vmem_mib * 1024 * 1024 <= V7X_VMEM_BYTES
    return pltpu.CompilerParams(dimension_semantics=semantics,
                                vmem_limit_bytes=vmem_mib * 1024 * 1024)


def _dot(a, b):
    return jnp.dot(a, b, preferred_element_type=F32)


def _rms_mod(x, g, shift, scale):
    ms = jnp.mean(x * x, axis=-1, keepdims=True)
    y = x * lax.rsqrt(ms + EPS) * g
    return y * (1.0 + scale) + shift


def _mod_kernel(c_ref, w_ref, b_ref, o_ref):
    c = c_ref[...]
    c_act = (c * jax.nn.sigmoid(c)).astype(BF16)
    o_ref[...] = _dot(c_act, w_ref[...].astype(BF16)) + b_ref[...]


def _modulation(c, w_mod, b_mod, tn=1024):
    depth, d, n = w_mod.shape
    bsz = c.shape[0]
    return pl.pallas_call(
        _mod_kernel,
        grid=(depth, n // tn),
        in_specs=[
            pl.BlockSpec((bsz, d), lambda l, j: (0, 0)),
            pl.BlockSpec((None, d, tn), lambda l, j: (l, 0, j)),
            pl.BlockSpec((None, 1, tn), lambda l, j: (l, 0, j)),
        ],
        out_specs=pl.BlockSpec((None, bsz, tn), lambda l, j: (l, 0, j)),
        out_shape=jax.ShapeDtypeStruct((depth, bsz, n), F32),
        compiler_params=_params(("arbitrary", "arbitrary"), 40),
        name="modulation",
    )(c, w_mod, b_mod.reshape(depth, 1, n))


def _norm_kernel(x_ref, mod_ref, g_ref, o_ref):
    m = mod_ref[...]
    o_ref[...] = _rms_mod(x_ref[...], g_ref[...], m[0:1], m[1:2]).astype(o_ref.dtype)


def _first_norm(x, mod, norm_g, seq, tm=512):
    n, d = x.shape
    tiles_per_seq = seq // tm
    return pl.pallas_call(
        _norm_kernel,
        grid=(n // tm,),
        in_specs=[
            pl.BlockSpec((tm, d), lambda i: (i, 0)),
            pl.BlockSpec((None, None, N_MOD, d), lambda i: (0, i // tiles_per_seq, 0, 0)),
            pl.BlockSpec((None, 1, d), lambda i: (0, 0, 0)),
        ],
        out_specs=pl.BlockSpec((tm, d), lambda i: (i, 0)),
        out_shape=jax.ShapeDtypeStruct((n, d), BF16),
        compiler_params=_params(("arbitrary",), 32),
        name="first_norm",
    )(x, mod, norm_g)


def _matmul_kernel(h_ref, w_ref, o_ref):
    o_ref[...] = _dot(h_ref[...], w_ref[...]).astype(o_ref.dtype)


def _in_proj(h, w_in, layer, tm=1024, tn=1024):
    n, d = h.shape
    d_in = w_in.shape[-1]
    return pl.pallas_call(
        _matmul_kernel,
        grid=(d_in // tn, n // tm),
        in_specs=[
            pl.BlockSpec((tm, d), lambda j, i: (i, 0)),
            pl.BlockSpec((None, d, tn), lambda j, i: (layer, 0, j)),
        ],
        out_specs=pl.BlockSpec((tm, tn), lambda j, i: (i, j)),
        out_shape=jax.ShapeDtypeStruct((n, d_in), BF16),
        compiler_params=_params(("arbitrary", "arbitrary"), 48),
        name="in_proj",
    )(h, w_in)


def _mixer_kernel(ax_ref, ag_ref, bv_ref, bbg_ref, bcg_ref, ga0_ref, ga1_ref, gb0_ref, gb1_ref,
                  caw_ref, cab_ref, wra_ref, bra_ref, wrx_ref, brx_ref, lam_ref, woa_ref,
                  cbw_ref, cbb_ref, wob_ref, o_ref,
                  xbuf, cbuf, a_s, h_s, hcar):
    ts, w = ax_ref.shape
    pad = V7X_SUBLANES
    groups = w // V7X_MXU_DIM

    @pl.when(pl.program_id(1) == 0)
    def _():
        xbuf[0:pad, :] = jnp.zeros((pad, w), F32)
        cbuf[0:pad, :] = jnp.zeros((pad, w), F32)
        hcar[...] = jnp.zeros_like(hcar)

    xbuf[pad:pad + ts, :] = ax_ref[...].astype(F32)
    xc = cab_ref[...]
    for j in range(RG_CONV):
        xc = xc + xbuf[pl.ds(pad - (RG_CONV - 1) + j, ts), :] * caw_ref[j:j + 1, :]
    xbuf[0:pad, :] = xbuf[ts:ts + pad, :]

    xcb = xc.astype(BF16)

    def gate(w_ref, b_ref):
        parts = [_dot(xcb[:, g * V7X_MXU_DIM:(g + 1) * V7X_MXU_DIM], w_ref[g])
                 for g in range(groups)]
        return jax.nn.sigmoid(jnp.concatenate(parts, axis=-1) + b_ref[...])

    r = gate(wra_ref, bra_ref)
    i = gate(wrx_ref, brx_ref)
    nl = -lam_ref[...]
    softplus = jnp.maximum(nl, 0.0) + jnp.log1p(jnp.exp(-jnp.abs(nl)))
    log_a = (-RG_C) * r * softplus
    a = jnp.exp(log_a)
    u = jnp.sqrt(-jnp.tanh(log_a) * (a * a + 1.0)) * (i * xc)
    a_s[...] = a
    h_s[...] = u

    row = lax.broadcasted_iota(jnp.int32, (V7X_SUBLANES, w), 0)

    def scan_group(k, carry):
        off = pl.multiple_of(k * V7X_SUBLANES, V7X_SUBLANES)
        ca = a_s[pl.ds(off, V7X_SUBLANES), :]
        cb = h_s[pl.ds(off, V7X_SUBLANES), :]
        d = 1
        while d < V7X_SUBLANES:
            keep = row >= d
            a_prev = jnp.where(keep, pltpu.roll(ca, d, 0), 1.0)
            b_prev = jnp.where(keep, pltpu.roll(cb, d, 0), 0.0)
            cb = ca * b_prev + cb
            ca = ca * a_prev
            d *= 2
        hh = ca * carry + cb
        h_s[pl.ds(off, V7X_SUBLANES), :] = hh
        return jnp.broadcast_to(hh[V7X_SUBLANES - 1:V7X_SUBLANES, :], (V7X_SUBLANES, w))

    hcar[...] = lax.fori_loop(0, ts // V7X_SUBLANES, scan_group, hcar[...], unroll=2)

    pa = (jax.nn.gelu(ag_ref[...].astype(F32), approximate=True) * h_s[...]).astype(BF16)
    ya = _dot(pa, woa_ref[...])

    cbuf[pad:pad + ts, :] = bcg_ref[...].astype(F32) * bv_ref[...].astype(F32)
    cv = cbb_ref[...]
    for j in range(SC_WIDTH):
        cv = cv + cbuf[pl.ds(pad - (SC_WIDTH - 1) + j, ts), :] * cbw_ref[j:j + 1, :]
    cbuf[0:pad, :] = cbuf[ts:ts + pad, :]
    pb = (bbg_ref[...].astype(F32) * cv).astype(BF16)
    yb = _dot(pb, wob_ref[...])

    for half, (ga_ref, gb_ref) in enumerate(((ga0_ref, gb0_ref), (ga1_ref, gb1_ref))):
        cols = slice(half * w, (half + 1) * w)
        o_ref[:, cols] = (jax.nn.sigmoid(ga_ref[...].astype(F32)) * ya[:, cols]
                          + jax.nn.sigmoid(gb_ref[...].astype(F32)) * yb[:, cols]).astype(o_ref.dtype)


def _mixer(z, p, layer, bsz, seq, ts=256):
    n = z.shape[0]
    w = p["conv_a_w"].shape[-1]
    d = p["w_out_a"].shape[-1]
    assert p["conv_b_w"].shape[-1] == w and d == 2 * w and w % V7X_MXU_DIM == 0
    tiles = seq // ts

    def zcol(c):
        return pl.BlockSpec((ts, w), lambda b, s: (b * tiles + s, c))

    def vec():
        return pl.BlockSpec((None, 1, w), lambda b, s: (layer, 0, 0))

    def full(shape):
        nd = len(shape)
        return pl.BlockSpec((None,) + shape, lambda b, s: (layer,) + (0,) * nd)

    groups = w // V7X_MXU_DIM
    return pl.pallas_call(
        _mixer_kernel,
        grid=(bsz, tiles),
        in_specs=[zcol(c) for c in range(9)] + [
            full((RG_CONV, w)), vec(),
            full((groups, V7X_MXU_DIM, V7X_MXU_DIM)), vec(),
            full((groups, V7X_MXU_DIM, V7X_MXU_DIM)), vec(),
            vec(), full((w, d)),
            full((SC_WIDTH, w)), vec(), full((w, d)),
        ],
        out_specs=pl.BlockSpec((ts, d), lambda b, s: (b * tiles + s, 0)),
        out_shape=jax.ShapeDtypeStruct((n, d), BF16),
        scratch_shapes=[
            pltpu.VMEM((ts + 2 * V7X_SUBLANES, w), F32),
            pltpu.VMEM((ts + 2 * V7X_SUBLANES, w), F32),
            pltpu.VMEM((ts, w), F32),
            pltpu.VMEM((ts, w), F32),
            pltpu.VMEM((V7X_SUBLANES, w), F32),
        ],
        compiler_params=_params(("arbitrary", "arbitrary"), 56),
        name="mixer",
    )(*([z] * 9), p["conv_a_w"], p["conv_a_b"], p["w_rg_a"], p["b_rg_a"], p["w_rg_x"], p["b_rg_x"],
      p["rg_lambda"], p["w_out_a"], p["conv_b_w"], p["conv_b_b"], p["w_out_b"])


def _top2_gates(logits):
    lane = lax.broadcasted_iota(jnp.int32, logits.shape, 1).astype(F32)
    n_lanes = float(logits.shape[-1])
    v1 = jnp.max(logits, axis=-1, keepdims=True)
    i1 = jnp.min(jnp.where(logits == v1, lane, n_lanes), axis=-1, keepdims=True)
    rest = jnp.where(lane == i1, -jnp.inf, logits)
    v2 = jnp.max(rest, axis=-1, keepdims=True)
    i2 = jnp.min(jnp.where(rest == v2, lane, n_lanes), axis=-1, keepdims=True)
    e = jnp.exp(v2 - v1)
    w1 = 1.0 / (1.0 + e)
    w2 = e / (1.0 + e)
    return jnp.where(lane == i1, w1, 0.0) + jnp.where(lane == i2, w2, 0.0)


def _out_proj_kernel(*refs, with_router):
    if with_router:
        m_ref, wo_ref, x_ref, mod_ref, g_ref, wr_ref, br_ref, xo_ref, ho_ref, gate_ref = refs
    else:
        m_ref, wo_ref, x_ref, mod_ref, g_ref, xo_ref, ho_ref = refs
    mod = mod_ref[...]
    xn = x_ref[...] + mod[2:3] * _dot(m_ref[...], wo_ref[...])
    xo_ref[...] = xn
    h2 = _rms_mod(xn, g_ref[...], mod[3:4], mod[4:5]).astype(BF16)
    ho_ref[...] = h2
    if with_router:
        gate_ref[...] = _top2_gates(_dot(h2, wr_ref[...]) + br_ref[...])


def _out_proj(merged, w_o, x, mod, norm_g, layer, seq, router=None, tm=512):
    n, d = x.shape
    tiles_per_seq = seq // tm
    in_specs = [
        pl.BlockSpec((tm, d), lambda i: (i, 0)),
        pl.BlockSpec((None, d, d), lambda i: (layer, 0, 0)),
        pl.BlockSpec((tm, d), lambda i: (i, 0)),
        pl.BlockSpec((None, None, N_MOD, d), lambda i: (layer, i // tiles_per_seq, 0, 0)),
        pl.BlockSpec((None, 1, d), lambda i: (layer, 0, 0)),
    ]
    out_specs = [pl.BlockSpec((tm, d), lambda i: (i, 0)), pl.BlockSpec((tm, d), lambda i: (i, 0))]
    out_shape = [jax.ShapeDtypeStruct((n, d), F32), jax.ShapeDtypeStruct((n, d), BF16)]
    args = [merged, w_o, x, mod, norm_g]
    if router is not None:
        w_r, b_r, j = router
        in_specs += [pl.BlockSpec((None, d, V7X_LANES), lambda i: (j, 0, 0)),
                     pl.BlockSpec((None, 1, V7X_LANES), lambda i: (j, 0, 0))]
        out_specs.append(pl.BlockSpec((tm, V7X_LANES), lambda i: (i, 0)))
        out_shape.append(jax.ShapeDtypeStruct((n, V7X_LANES), F32))
        args += [w_r, b_r]
    return pl.pallas_call(
        functools.partial(_out_proj_kernel, with_router=router is not None),
        grid=(n // tm,),
        in_specs=in_specs, out_specs=out_specs, out_shape=out_shape,
        input_output_aliases={2: 0},
        compiler_params=_params(("arbitrary",), 56),
        name="out_proj",
    )(*args)


def _ffn_kernel(*refs, gated, last):
    if gated:
        h_ref, wg_ref, wu_ref, wd_ref, x_ref, mod_ref, g_ref, nmod_ref, gate_ref, xo_ref, ho_ref = refs
    else:
        h_ref, wg_ref, wu_ref, wd_ref, x_ref, mod_ref, g_ref, nmod_ref, xo_ref, ho_ref = refs
    e = pl.program_id(1)
    j = pl.program_id(2)
    first = jnp.logical_and(e == 0, j == 0)
    final = jnp.logical_and(e == pl.num_programs(1) - 1, j == pl.num_programs(2) - 1)

    @pl.when(first)
    def _():
        xo_ref[...] = jnp.zeros_like(xo_ref)

    h = h_ref[...]
    a = _dot(h, wg_ref[...])
    act = (a * jax.nn.sigmoid(a)) * _dot(h, wu_ref[...])
    if gated:
        gates = gate_ref[...]
        lane = lax.broadcasted_iota(jnp.int32, gates.shape, 1)
        act = act * jnp.sum(jnp.where(lane == e, gates, 0.0), axis=-1, keepdims=True)
    xo_ref[...] += _dot(act.astype(BF16), wd_ref[...])

    @pl.when(final)
    def _():
        xn = x_ref[...] + mod_ref[...][5:6] * xo_ref[...]
        xo_ref[...] = xn
        if last:
            zero = jnp.zeros_like(g_ref[...])
            ho_ref[...] = _rms_mod(xn, g_ref[...], zero, zero).astype(ho_ref.dtype)
        else:
            nmod = nmod_ref[...]
            ho_ref[...] = _rms_mod(xn, g_ref[...], nmod[0:1], nmod[1:2]).astype(ho_ref.dtype)


def _ffn(h, wg, wu, wd, w_base, n_experts, x, mod, layer, next_g, next_g_index, next_layer, seq,
         gates=None, last=False, tm=512, tf=512):
    n, d = x.shape
    f = wg.shape[-1]
    tiles_per_seq = seq // tm
    gated = gates is not None
    in_specs = [
        pl.BlockSpec((tm, d), lambda i, e, j: (i, 0)),
        pl.BlockSpec((None, d, tf), lambda i, e, j: (w_base + e, 0, j)),
        pl.BlockSpec((None, d, tf), lambda i, e, j: (w_base + e, 0, j)),
        pl.BlockSpec((None, tf, d), lambda i, e, j: (w_base + e, j, 0)),
        pl.BlockSpec((tm, d), lambda i, e, j: (i, 0)),
        pl.BlockSpec((None, None, N_MOD, d), lambda i, e, j: (layer, i // tiles_per_seq, 0, 0)),
        pl.BlockSpec((None, 1, d), lambda i, e, j: (next_g_index, 0, 0)),
        pl.BlockSpec((None, None, N_MOD, d), lambda i, e, j: (next_layer, i // tiles_per_seq, 0, 0)),
    ]
    args = [h, wg, wu, wd, x, mod, next_g, mod]
    if gated:
        in_specs.append(pl.BlockSpec((tm, V7X_LANES), lambda i, e, j: (i, 0)))
        args.append(gates)
    return pl.pallas_call(
        functools.partial(_ffn_kernel, gated=gated, last=last),
        grid=(n // tm, n_experts, f // tf),
        in_specs=in_specs,
        out_specs=[pl.BlockSpec((tm, d), lambda i, e, j: (i, 0)),
                   pl.BlockSpec((tm, d), lambda i, e, j: (i, 0))],
        out_shape=[jax.ShapeDtypeStruct((n, d), F32),
                   jax.ShapeDtypeStruct((n, d), F32 if last else BF16)],
        input_output_aliases={4: 0},
        compiler_params=_params(("arbitrary", "arbitrary", "arbitrary"), 56),
        name="ffn",
    )(*args)


def _block_diag_groups(w_heads):
    depth, heads, hd, _ = w_heads.shape
    per = V7X_MXU_DIM // hd
    groups = heads // per
    wg = w_heads.reshape(depth, groups, per, hd, hd)
    eye = jnp.eye(per, dtype=w_heads.dtype)
    out = jnp.einsum("lgpde,pq->lgpdqe", wg, eye)
    return out.reshape(depth, groups, V7X_MXU_DIM, V7X_MXU_DIM)


def kernel(x, c, w_mod, b_mod, norm1_g, norm2_g, w_in, conv_a_w, conv_a_b, w_rg_a, b_rg_a, w_rg_x, b_rg_x, rg_lambda, w_out_a, conv_b_w, conv_b_b, w_out_b, w_o, w_ff_gate, w_ff_up, w_ff_down, w_router, b_router, w_e_gate, w_e_up, w_e_down, final_g):
    bsz, seq, d = x.shape
    depth = w_in.shape[0]
    n = bsz * seq
    n_moe, n_experts = w_router.shape[0], w_router.shape[-1]
    w_rnn = conv_a_w.shape[-1]

    w_in_b = w_in.astype(BF16)
    w_o_b = w_o.astype(BF16)
    mixer_p = dict(
        conv_a_w=conv_a_w, conv_a_b=conv_a_b.reshape(depth, 1, w_rnn),
        w_rg_a=_block_diag_groups(w_rg_a).astype(BF16), b_rg_a=b_rg_a.reshape(depth, 1, w_rnn),
        w_rg_x=_block_diag_groups(w_rg_x).astype(BF16), b_rg_x=b_rg_x.reshape(depth, 1, w_rnn),
        rg_lambda=rg_lambda.reshape(depth, 1, w_rnn), w_out_a=w_out_a.astype(BF16),
        conv_b_w=conv_b_w, conv_b_b=conv_b_b.reshape(depth, 1, w_rnn), w_out_b=w_out_b.astype(BF16),
    )
    wg_d, wu_d, wd_d = (w.astype(BF16) for w in (w_ff_gate, w_ff_up, w_ff_down))
    f_e = w_e_gate.shape[-1]
    wg_e = w_e_gate.astype(BF16).reshape(n_moe * n_experts, d, f_e)
    wu_e = w_e_up.astype(BF16).reshape(n_moe * n_experts, d, f_e)
    wd_e = w_e_down.astype(BF16).reshape(n_moe * n_experts, f_e, d)
    w_r = jnp.pad(w_router, ((0, 0), (0, 0), (0, V7X_LANES - n_experts))).astype(BF16)
    b_r = jnp.pad(b_router, ((0, 0), (0, V7X_LANES - n_experts)),
                  constant_values=ROUTER_PAD_BIAS).reshape(n_moe, 1, V7X_LANES)
    norm1 = norm1_g.reshape(depth, 1, d)
    norm2 = norm2_g.reshape(depth, 1, d)
    final = final_g.reshape(1, 1, d)

    mod = _modulation(c, w_mod, b_mod).reshape(depth, bsz, N_MOD, d)
    xs = x.reshape(n, d)
    h = _first_norm(xs, mod, norm1, seq)
    for l in range(depth):
        z = _in_proj(h, w_in_b, l)
        merged = _mixer(z, mixer_p, l, bsz, seq)
        is_last = l == depth - 1
        nxt = (final, 0, l) if is_last else (norm1, l + 1, l + 1)
        j = l // 2
        if l % 2 == 0:
            xs, h2 = _out_proj(merged, w_o_b, xs, mod, norm2, l, seq)
            xs, h = _ffn(h2, wg_d, wu_d, wd_d, j, 1, xs, mod, l, *nxt, seq, last=is_last)
        else:
            xs, h2, gates = _out_proj(merged, w_o_b, xs, mod, norm2, l, seq, router=(w_r, b_r, j))
            xs, h = _ffn(h2, wg_e, wu_e, wd_e, j * n_experts, n_experts, xs, mod, l, *nxt, seq,
                         gates=gates, last=is_last)
    return h.reshape(bsz, seq, d)
```

```python
import functools

import jax
import jax.numpy as jnp
from jax import lax
from jax.experimental import pallas as pl
from jax.experimental.pallas import tpu as pltpu

F32 = jnp.float32
BF16 = jnp.bfloat16

V7X_LANES = 128
V7X_SUBLANES = 8
V7X_MXU_DIM = 256
V7X_VMEM_BYTES = 64 * 1024 * 1024

RG_C = 8.0
RG_CONV = 4
SC_WIDTH = 3
N_MOD = 6
EPS = 1e-6
ROUTER_PAD_BIAS = -1e30


def _params(semantics, vmem_mib):
    assert vmem_mib * 1024 * 1024 <= V7X_VMEM_BYTES
    return pltpu.CompilerParams(dimension_semantics=semantics,
                                vmem_limit_bytes=vmem_mib * 1024 * 1024)


def _dot(a, b):
    return jnp.dot(a, b, preferred_element_type=F32)


def _rms_mod(x, g, shift, scale):
    ms = jnp.mean(x * x, axis=-1, keepdims=True)
    y = x * lax.rsqrt(ms + EPS) * g
    return y * (1.0 + scale) + shift


def _mod_kernel(c_ref, w_ref, b_ref, o_ref):
    c = c_ref[...]
    c_act = (c * jax.nn.sigmoid(c)).astype(BF16)
    o_ref[...] = _dot(c_act, w_ref[...].astype(BF16)) + b_ref[...]


def _modulation(c, w_mod, b_mod, tn=1024):
    depth, d, n = w_mod.shape
    bsz = c.shape[0]
    return pl.pallas_call(
        _mod_kernel,
        grid=(depth, n // tn),
        in_specs=[
            pl.BlockSpec((bsz, d), lambda l, j: (0, 0)),
            pl.BlockSpec((None, d, tn), lambda l, j: (l, 0, j)),
            pl.BlockSpec((None, 1, tn), lambda l, j: (l, 0, j)),
        ],
        out_specs=pl.BlockSpec((None, bsz, tn), lambda l, j: (l, 0, j)),
        out_shape=jax.ShapeDtypeStruct((depth, bsz, n), F32),
        compiler_params=_params(("arbitrary", "arbitrary"), 40),
        name="modulation",
    )(c, w_mod, b_mod.reshape(depth, 1, n))


def _norm_kernel(x_ref, mod_ref, g_ref, o_ref):
    m = mod_ref[...]
    o_ref[...] = _rms_mod(x_ref[...], g_ref[...], m[0:1], m[1:2]).astype(o_ref.dtype)


def _first_norm(x, mod, norm_g, seq, tm=512):
    n, d = x.shape
    tiles_per_seq = seq // tm
    return pl.pallas_call(
        _norm_kernel,
        grid=(n // tm,),
        in_specs=[
            pl.BlockSpec((tm, d), lambda i: (i, 0)),
            pl.BlockSpec((None, None, N_MOD, d), lambda i: (0, i // tiles_per_seq, 0, 0)),
            pl.BlockSpec((None, 1, d), lambda i: (0, 0, 0)),
        ],
        out_specs=pl.BlockSpec((tm, d), lambda i: (i, 0)),
        out_shape=jax.ShapeDtypeStruct((n, d), BF16),
        compiler_params=_params(("arbitrary",), 32),
        name="first_norm",
    )(x, mod, norm_g)


def _matmul_kernel(h_ref, w_ref, o_ref):
    o_ref[...] = _dot(h_ref[...], w_ref[...]).astype(o_ref.dtype)


def _in_proj(h, w_in, layer, tm=1024, tn=1024):
    n, d = h.shape
    d_in = w_in.shape[-1]
    return pl.pallas_call(
        _matmul_kernel,
        grid=(d_in // tn, n // tm),
        in_specs=[
            pl.BlockSpec((tm, d), lambda j, i: (i, 0)),
            pl.BlockSpec((None, d, tn), lambda j, i: (layer, 0, j)),
        ],
        out_specs=pl.BlockSpec((tm, tn), lambda j, i: (i, j)),
        out_shape=jax.ShapeDtypeStruct((n, d_in), BF16),
        compiler_params=_params(("arbitrary", "arbitrary"), 48),
        name="in_proj",
    )(h, w_in)


def _mixer_kernel(ax_ref, ag_ref, bv_ref, bbg_ref, bcg_ref, ga0_ref, ga1_ref, gb0_ref, gb1_ref,
                  caw_ref, cab_ref, wra_ref, bra_ref, wrx_ref, brx_ref, lam_ref, woa_ref,
                  cbw_ref, cbb_ref, wob_ref, o_ref,
                  xbuf, cbuf, a_s, h_s, hcar):
    ts, w = ax_ref.shape
    pad = V7X_SUBLANES
    groups = w // V7X_MXU_DIM

    @pl.when(pl.program_id(1) == 0)
    def _():
        xbuf[0:pad, :] = jnp.zeros((pad, w), F32)
        cbuf[0:pad, :] = jnp.zeros((pad, w), F32)
        hcar[...] = jnp.zeros_like(hcar)

    xbuf[pad:pad + ts, :] = ax_ref[...].astype(F32)
    xc = cab_ref[...]
    for j in range(RG_CONV):
        xc = xc + xbuf[pl.ds(pad - (RG_CONV - 1) + j, ts), :] * caw_ref[j:j + 1, :]
    xbuf[0:pad, :] = xbuf[ts:ts + pad, :]

    xcb = xc.astype(BF16)

    def gate(w_ref, b_ref):
        parts = [_dot(xcb[:, g * V7X_MXU_DIM:(g + 1) * V7X_MXU_DIM], w_ref[g])
                 for g in range(groups)]
        return jax.nn.sigmoid(jnp.concatenate(parts, axis=-1) + b_ref[...])

    r = gate(wra_ref, bra_ref)
    i = gate(wrx_ref, brx_ref)
    nl = -lam_ref[...]
    softplus = jnp.maximum(nl, 0.0) + jnp.log1p(jnp.exp(-jnp.abs(nl)))
    log_a = (-RG_C) * r * softplus
    a = jnp.exp(log_a)
    u = jnp.sqrt(-jnp.tanh(log_a) * (a * a + 1.0)) * (i * xc)
    a_s[...] = a
    h_s[...] = u

    row = lax.broadcasted_iota(jnp.int32, (V7X_SUBLANES, w), 0)

    def scan_group(k, carry):
        off = pl.multiple_of(k * V7X_SUBLANES, V7X_SUBLANES)
        ca = a_s[pl.ds(off, V7X_SUBLANES), :]
        cb = h_s[pl.ds(off, V7X_SUBLANES), :]
        d = 1
        while d < V7X_SUBLANES:
            keep = row >= d
            a_prev = jnp.where(keep, pltpu.roll(ca, d, 0), 1.0)
            b_prev = jnp.where(keep, pltpu.roll(cb, d, 0), 0.0)
            cb = ca * b_prev + cb
            ca = ca * a_prev
            d *= 2
        hh = ca * carry + cb
        h_s[pl.ds(off, V7X_SUBLANES), :] = hh
        return jnp.broadcast_to(hh[V7X_SUBLANES - 1:V7X_SUBLANES, :], (V7X_SUBLANES, w))

    hcar[...] = lax.fori_loop(0, ts // V7X_SUBLANES, scan_group, hcar[...], unroll=2)

    pa = (jax.nn.gelu(ag_ref[...].astype(F32), approximate=True) * h_s[...]).astype(BF16)
    ya = _dot(pa, woa_ref[...])

    cbuf[pad:pad + ts, :] = bcg_ref[...].astype(F32) * bv_ref[...].astype(F32)
    cv = cbb_ref[...]
    for j in range(SC_WIDTH):
        cv = cv + cbuf[pl.ds(pad - (SC_WIDTH - 1) + j, ts), :] * cbw_ref[j:j + 1, :]
    cbuf[0:pad, :] = cbuf[ts:ts + pad, :]
    pb = (bbg_ref[...].astype(F32) * cv).astype(BF16)
    yb = _dot(pb, wob_ref[...])

    for half, (ga_ref, gb_ref) in enumerate(((ga0_ref, gb0_ref), (ga1_ref, gb1_ref))):
        cols = slice(half * w, (half + 1) * w)
        o_ref[:, cols] = (jax.nn.sigmoid(ga_ref[...].astype(F32)) * ya[:, cols]
                          + jax.nn.sigmoid(gb_ref[...].astype(F32)) * yb[:, cols]).astype(o_ref.dtype)


def _mixer(z, p, layer, bsz, seq, ts=256):
    n = z.shape[0]
    w = p["conv_a_w"].shape[-1]
    d = p["w_out_a"].shape[-1]
    assert p["conv_b_w"].shape[-1] == w and d == 2 * w and w % V7X_MXU_DIM == 0
    tiles = seq // ts

    def zcol(c):
        return pl.BlockSpec((ts, w), lambda b, s: (b * tiles + s, c))

    def vec():
        return pl.BlockSpec((None, 1, w), lambda b, s: (layer, 0, 0))

    def full(shape):
        nd = len(shape)
        return pl.BlockSpec((None,) + shape, lambda b, s: (layer,) + (0,) * nd)

    groups = w // V7X_MXU_DIM
    return pl.pallas_call(
        _mixer_kernel,
        grid=(bsz, tiles),
        in_specs=[zcol(c) for c in range(9)] + [
            full((RG_CONV, w)), vec(),
            full((groups, V7X_MXU_DIM, V7X_MXU_DIM)), vec(),
            full((groups, V7X_MXU_DIM, V7X_MXU_DIM)), vec(),
            vec(), full((w, d)),
            full((SC_WIDTH, w)), vec(), full((w, d)),
        ],
        out_specs=pl.BlockSpec((ts, d), lambda b, s: (b * tiles + s, 0)),
        out_shape=jax.ShapeDtypeStruct((n, d), BF16),
        scratch_shapes=[
            pltpu.VMEM((ts + 2 * V7X_SUBLANES, w), F32),
            pltpu.VMEM((ts + 2 * V7X_SUBLANES, w), F32),
            pltpu.VMEM((ts, w), F32),
            pltpu.VMEM((ts, w), F32),
            pltpu.VMEM((V7X_SUBLANES, w), F32),
        ],
        compiler_params=_params(("arbitrary", "arbitrary"), 56),
        name="mixer",
    )(*([z] * 9), p["conv_a_w"], p["conv_a_b"], p["w_rg_a"], p["b_rg_a"], p["w_rg_x"], p["b_rg_x"],
      p["rg_lambda"], p["w_out_a"], p["conv_b_w"], p["conv_b_b"], p["w_out_b"])


ROUTE_FIRST, ROUTE_SECOND, ROUTE_W_FIRST, ROUTE_W_SECOND = 0, 1, 2, 3


def _top2_route(logits):
    lane = lax.broadcasted_iota(jnp.int32, logits.shape, 1).astype(F32)
    n_lanes = float(logits.shape[-1])
    v1 = jnp.max(logits, axis=-1, keepdims=True)
    i1 = jnp.min(jnp.where(logits == v1, lane, n_lanes), axis=-1, keepdims=True)
    rest = jnp.where(lane == i1, -jnp.inf, logits)
    v2 = jnp.max(rest, axis=-1, keepdims=True)
    i2 = jnp.min(jnp.where(rest == v2, lane, n_lanes), axis=-1, keepdims=True)
    e = jnp.exp(v2 - v1)
    w1 = 1.0 / (1.0 + e)
    w2 = e / (1.0 + e)
    out = jnp.where(lane == ROUTE_FIRST, i1, 0.0)
    out = jnp.where(lane == ROUTE_SECOND, i2, out)
    out = jnp.where(lane == ROUTE_W_FIRST, w1, out)
    return jnp.where(lane == ROUTE_W_SECOND, w2, out)


def _out_proj_kernel(*refs, with_router):
    if with_router:
        m_ref, wo_ref, x_ref, mod_ref, g_ref, wr_ref, br_ref, xo_ref, ho_ref, route_ref = refs
    else:
        m_ref, wo_ref, x_ref, mod_ref, g_ref, xo_ref, ho_ref = refs
    mod = mod_ref[...]
    xn = x_ref[...] + mod[2:3] * _dot(m_ref[...], wo_ref[...])
    xo_ref[...] = xn
    h2 = _rms_mod(xn, g_ref[...], mod[3:4], mod[4:5])
    ho_ref[...] = h2.astype(ho_ref.dtype)
    if with_router:
        route_ref[...] = _top2_route(_dot(h2.astype(BF16), wr_ref[...]) + br_ref[...])


def _out_proj(merged, w_o, x, mod, norm_g, layer, seq, router=None, tm=512):
    n, d = x.shape
    tiles_per_seq = seq // tm
    in_specs = [
        pl.BlockSpec((tm, d), lambda i: (i, 0)),
        pl.BlockSpec((None, d, d), lambda i: (layer, 0, 0)),
        pl.BlockSpec((tm, d), lambda i: (i, 0)),
        pl.BlockSpec((None, None, N_MOD, d), lambda i: (layer, i // tiles_per_seq, 0, 0)),
        pl.BlockSpec((None, 1, d), lambda i: (layer, 0, 0)),
    ]
    out_specs = [pl.BlockSpec((tm, d), lambda i: (i, 0)), pl.BlockSpec((tm, d), lambda i: (i, 0))]
    out_shape = [jax.ShapeDtypeStruct((n, d), F32),
                 jax.ShapeDtypeStruct((n, d), BF16 if router is None else F32)]
    args = [merged, w_o, x, mod, norm_g]
    if router is not None:
        w_r, b_r, j = router
        in_specs += [pl.BlockSpec((None, d, V7X_LANES), lambda i: (j, 0, 0)),
                     pl.BlockSpec((None, 1, V7X_LANES), lambda i: (j, 0, 0))]
        out_specs.append(pl.BlockSpec((tm, V7X_LANES), lambda i: (i, 0)))
        out_shape.append(jax.ShapeDtypeStruct((n, V7X_LANES), F32))
        args += [w_r, b_r]
    return pl.pallas_call(
        functools.partial(_out_proj_kernel, with_router=router is not None),
        grid=(n // tm,),
        in_specs=in_specs, out_specs=out_specs, out_shape=out_shape,
        input_output_aliases={2: 0},
        compiler_params=_params(("arbitrary",), 56),
        name="out_proj",
    )(*args)


def _residual_norm(x, gate, f, g, nmod, last):
    xn = x + gate * f
    if last:
        zero = jnp.zeros_like(g)
        return xn, _rms_mod(xn, g, zero, zero)
    return xn, _rms_mod(xn, g, nmod[0:1], nmod[1:2])


def _ffn_kernel(h_ref, wg_ref, wu_ref, wd_ref, x_ref, mod_ref, g_ref, nmod_ref, xo_ref, ho_ref, *, last):
    j = pl.program_id(1)

    @pl.when(j == 0)
    def _():
        xo_ref[...] = jnp.zeros_like(xo_ref)

    h = h_ref[...]
    a = _dot(h, wg_ref[...])
    act = (a * jax.nn.sigmoid(a)) * _dot(h, wu_ref[...])
    xo_ref[...] += _dot(act.astype(BF16), wd_ref[...])

    @pl.when(j == pl.num_programs(1) - 1)
    def _():
        xn, hn = _residual_norm(x_ref[...], mod_ref[...][5:6], xo_ref[...], g_ref[...], nmod_ref[...], last)
        xo_ref[...] = xn
        ho_ref[...] = hn.astype(ho_ref.dtype)


def _ffn(h, wg, wu, wd, w_index, x, mod, layer, next_g, next_g_index, next_layer, seq,
         last=False, tm=512, tf=512):
    n, d = x.shape
    f = wg.shape[-1]
    tiles_per_seq = seq // tm
    return pl.pallas_call(
        functools.partial(_ffn_kernel, last=last),
        grid=(n // tm, f // tf),
        in_specs=[
            pl.BlockSpec((tm, d), lambda i, j: (i, 0)),
            pl.BlockSpec((None, d, tf), lambda i, j: (w_index, 0, j)),
            pl.BlockSpec((None, d, tf), lambda i, j: (w_index, 0, j)),
            pl.BlockSpec((None, tf, d), lambda i, j: (w_index, j, 0)),
            pl.BlockSpec((tm, d), lambda i, j: (i, 0)),
            pl.BlockSpec((None, None, N_MOD, d), lambda i, j: (layer, i // tiles_per_seq, 0, 0)),
            pl.BlockSpec((None, 1, d), lambda i, j: (next_g_index, 0, 0)),
            pl.BlockSpec((None, None, N_MOD, d), lambda i, j: (next_layer, i // tiles_per_seq, 0, 0)),
        ],
        out_specs=[pl.BlockSpec((tm, d), lambda i, j: (i, 0)),
                   pl.BlockSpec((tm, d), lambda i, j: (i, 0))],
        out_shape=[jax.ShapeDtypeStruct((n, d), F32),
                   jax.ShapeDtypeStruct((n, d), F32 if last else BF16)],
        input_output_aliases={4: 0},
        compiler_params=_params(("arbitrary", "arbitrary"), 56),
        name="ffn",
    )(h, wg, wu, wd, x, mod, next_g, mod)


TOP_K = 2


def _route_plan(route, n_experts, tm):
    n = route.shape[0]
    n_tiles = TOP_K * n // tm + n_experts - 1
    choice = route[:, ROUTE_FIRST:ROUTE_SECOND + 1].astype(jnp.int32)
    member = jnp.sum(choice[:, :, None] == jnp.arange(n_experts)[None, None, :], axis=1,
                     dtype=jnp.int32)
    rank = jnp.cumsum(member, axis=0) - member
    counts = rank[-1] + member[-1]
    padded = (counts + tm - 1) // tm * tm
    ends = jnp.cumsum(padded)
    dest = jnp.take_along_axis((ends - padded)[None, :] + rank, choice, axis=1)
    n_used = ends[-1] // tm
    tile = jnp.minimum(jnp.arange(n_tiles), n_used - 1)
    tile_expert = jnp.sum(tile[:, None] * tm >= ends[None, :], axis=1, dtype=jnp.int32)
    return dest.reshape(-1).astype(jnp.int32), tile_expert, n_used.reshape(1).astype(jnp.int32), n_tiles


def _dispatch_kernel(dest_ref, h_ref, init_ref, o_ref, sem):
    del init_ref
    tt = h_ref.shape[0]
    base = pl.program_id(0) * tt

    def row_copy(r, k):
        row = dest_ref[TOP_K * (base + r) + k]
        return pltpu.make_async_copy(h_ref.at[pl.ds(r, 1), :], o_ref.at[pl.ds(row, 1), :], sem)

    @pl.loop(0, tt)
    def _(r):
        for k in range(TOP_K):
            row_copy(r, k).start()

    @pl.loop(0, tt)
    def _(r):
        for k in range(TOP_K):
            row_copy(r, k).wait()


def _dispatch(h, dest, n_rows, tt=256):
    n, d = h.shape
    return pl.pallas_call(
        _dispatch_kernel,
        grid_spec=pltpu.PrefetchScalarGridSpec(
            num_scalar_prefetch=1,
            grid=(n // tt,),
            in_specs=[pl.BlockSpec((tt, d), lambda i, dest: (i, 0)),
                      pl.BlockSpec(memory_space=pl.ANY)],
            out_specs=pl.BlockSpec(memory_space=pl.ANY),
            scratch_shapes=[pltpu.SemaphoreType.DMA(())],
        ),
        out_shape=jax.ShapeDtypeStruct((n_rows, d), h.dtype),
        input_output_aliases={2: 0},
        compiler_params=_params(("arbitrary",), 32),
        name="dispatch",
    )(dest, h, jnp.zeros((n_rows, d), h.dtype))


def _expert_kernel(te_ref, nu_ref, x_ref, wg_ref, wu_ref, wd_ref, y_ref, xb):
    del te_ref
    j = pl.program_id(1)
    used = pl.program_id(0) < nu_ref[0]

    @pl.when(jnp.logical_and(jnp.logical_not(used), j == 0))
    def _():
        y_ref[...] = jnp.zeros_like(y_ref)

    @pl.when(used)
    def _():
        @pl.when(j == 0)
        def _():
            xb[...] = x_ref[...].astype(BF16)
            y_ref[...] = jnp.zeros_like(y_ref)

        h = xb[...]
        a = _dot(h, wg_ref[...])
        act = (a * jax.nn.sigmoid(a)) * _dot(h, wu_ref[...])
        y_ref[...] += _dot(act.astype(BF16), wd_ref[...])


def _expert_ffn(xs, wg, wu, wd, w_base, tile_expert, n_used, n_tiles, tm, tf=512):
    d = xs.shape[-1]
    f = wg.shape[-1]
    n_j = f // tf

    def row(i, j, te, nu):
        return (jnp.minimum(i, nu[0] - 1), 0)

    def col(i, j, nu):
        return jnp.where(i < nu[0], j, n_j - 1)

    return pl.pallas_call(
        _expert_kernel,
        grid_spec=pltpu.PrefetchScalarGridSpec(
            num_scalar_prefetch=2,
            grid=(n_tiles, n_j),
            in_specs=[
                pl.BlockSpec((tm, d), row),
                pl.BlockSpec((None, d, tf), lambda i, j, te, nu: (w_base + te[i], 0, col(i, j, nu))),
                pl.BlockSpec((None, d, tf), lambda i, j, te, nu: (w_base + te[i], 0, col(i, j, nu))),
                pl.BlockSpec((None, tf, d), lambda i, j, te, nu: (w_base + te[i], col(i, j, nu), 0)),
            ],
            out_specs=pl.BlockSpec((tm, d), lambda i, j, te, nu: (i, 0)),
            scratch_shapes=[pltpu.VMEM((tm, d), BF16)],
        ),
        out_shape=jax.ShapeDtypeStruct((n_tiles * tm, d), F32),
        compiler_params=_params(("arbitrary", "arbitrary"), 56),
        name="expert_ffn",
    )(tile_expert, n_used, xs, wg, wu, wd)


def _combine_kernel(dest_ref, y_ref, x_ref, route_ref, mod_ref, g_ref, nmod_ref, *rest, last):
    if last:
        ho_ref, ybuf, sem = rest
    else:
        xo_ref, ho_ref, ybuf, sem = rest
    tc = x_ref.shape[0]
    base = pl.program_id(0) * tc

    def row_copy(r, k):
        row = dest_ref[TOP_K * (base + r) + k]
        return pltpu.make_async_copy(y_ref.at[pl.ds(row, 1), :], ybuf.at[k, pl.ds(r, 1), :], sem.at[k])

    @pl.loop(0, tc)
    def _(r):
        for k in range(TOP_K):
            row_copy(r, k).start()

    @pl.loop(0, tc)
    def _(r):
        for k in range(TOP_K):
            row_copy(r, k).wait()

    route = route_ref[...]
    f = (route[:, ROUTE_W_FIRST:ROUTE_W_FIRST + 1] * ybuf[0]
         + route[:, ROUTE_W_SECOND:ROUTE_W_SECOND + 1] * ybuf[1])
    xn, hn = _residual_norm(x_ref[...], mod_ref[...][5:6], f, g_ref[...], nmod_ref[...], last)
    if not last:
        xo_ref[...] = xn
    ho_ref[...] = hn.astype(ho_ref.dtype)


def _combine(y, dest, x, route, mod, layer, next_g, next_g_index, next_layer, seq, last, tc=256):
    n, d = x.shape
    tiles_per_seq = seq // tc
    tok = pl.BlockSpec((tc, d), lambda i, dest: (i, 0))
    out_specs = [tok] if last else [tok, tok]
    out_shape = ([jax.ShapeDtypeStruct((n, d), F32)] if last else
                 [jax.ShapeDtypeStruct((n, d), F32), jax.ShapeDtypeStruct((n, d), BF16)])
    return pl.pallas_call(
        functools.partial(_combine_kernel, last=last),
        grid_spec=pltpu.PrefetchScalarGridSpec(
            num_scalar_prefetch=1,
            grid=(n // tc,),
            in_specs=[
                pl.BlockSpec(memory_space=pl.ANY),
                tok,
                pl.BlockSpec((tc, V7X_LANES), lambda i, dest: (i, 0)),
                pl.BlockSpec((None, None, N_MOD, d), lambda i, dest: (layer, i // tiles_per_seq, 0, 0)),
                pl.BlockSpec((None, 1, d), lambda i, dest: (next_g_index, 0, 0)),
                pl.BlockSpec((None, None, N_MOD, d), lambda i, dest: (next_layer, i // tiles_per_seq, 0, 0)),
            ],
            out_specs=out_specs,
            scratch_shapes=[pltpu.VMEM((TOP_K, tc, d), F32), pltpu.SemaphoreType.DMA((TOP_K,))],
        ),
        out_shape=out_shape,
        input_output_aliases={} if last else {2: 0},
        compiler_params=_params(("arbitrary",), 48),
        name="combine",
    )(dest, y, x, route, mod, next_g, mod)


def _moe(h2, route, wg, wu, wd, w_base, n_experts, x, mod, layer, next_g, next_g_index, next_layer,
         seq, last, tm=512):
    dest, tile_expert, n_used, n_tiles = _route_plan(route, n_experts, tm)
    xs = _dispatch(h2, dest, n_tiles * tm)
    y = _expert_ffn(xs, wg, wu, wd, w_base, tile_expert, n_used, n_tiles, tm)
    return _combine(y, dest, x, route, mod, layer, next_g, next_g_index, next_layer, seq, last)


def _block_diag_groups(w_heads):
    depth, heads, hd, _ = w_heads.shape
    per = V7X_MXU_DIM // hd
    groups = heads // per
    wg = w_heads.reshape(depth, groups, per, hd, hd)
    eye = jnp.eye(per, dtype=w_heads.dtype)
    out = jnp.einsum("lgpde,pq->lgpdqe", wg, eye)
    return out.reshape(depth, groups, V7X_MXU_DIM, V7X_MXU_DIM)


def kernel(x, c, w_mod, b_mod, norm1_g, norm2_g, w_in, conv_a_w, conv_a_b, w_rg_a, b_rg_a, w_rg_x, b_rg_x, rg_lambda, w_out_a, conv_b_w, conv_b_b, w_out_b, w_o, w_ff_gate, w_ff_up, w_ff_down, w_router, b_router, w_e_gate, w_e_up, w_e_down, final_g):
    bsz, seq, d = x.shape
    depth = w_in.shape[0]
    n = bsz * seq
    n_moe, n_experts = w_router.shape[0], w_router.shape[-1]
    w_rnn = conv_a_w.shape[-1]

    w_in_b = w_in.astype(BF16)
    w_o_b = w_o.astype(BF16)
    mixer_p = dict(
        conv_a_w=conv_a_w, conv_a_b=conv_a_b.reshape(depth, 1, w_rnn),
        w_rg_a=_block_diag_groups(w_rg_a).astype(BF16), b_rg_a=b_rg_a.reshape(depth, 1, w_rnn),
        w_rg_x=_block_diag_groups(w_rg_x).astype(BF16), b_rg_x=b_rg_x.reshape(depth, 1, w_rnn),
        rg_lambda=rg_lambda.reshape(depth, 1, w_rnn), w_out_a=w_out_a.astype(BF16),
        conv_b_w=conv_b_w, conv_b_b=conv_b_b.reshape(depth, 1, w_rnn), w_out_b=w_out_b.astype(BF16),
    )
    wg_d, wu_d, wd_d = (w.astype(BF16) for w in (w_ff_gate, w_ff_up, w_ff_down))
    f_e = w_e_gate.shape[-1]
    wg_e = w_e_gate.astype(BF16).reshape(n_moe * n_experts, d, f_e)
    wu_e = w_e_up.astype(BF16).reshape(n_moe * n_experts, d, f_e)
    wd_e = w_e_down.astype(BF16).reshape(n_moe * n_experts, f_e, d)
    w_r = jnp.pad(w_router, ((0, 0), (0, 0), (0, V7X_LANES - n_experts))).astype(BF16)
    b_r = jnp.pad(b_router, ((0, 0), (0, V7X_LANES - n_experts)),
                  constant_values=ROUTER_PAD_BIAS).reshape(n_moe, 1, V7X_LANES)
    norm1 = norm1_g.reshape(depth, 1, d)
    norm2 = norm2_g.reshape(depth, 1, d)
    final = final_g.reshape(1, 1, d)

    mod = _modulation(c, w_mod, b_mod).reshape(depth, bsz, N_MOD, d)
    xs = x.reshape(n, d)
    h = _first_norm(xs, mod, norm1, seq)
    for l in range(depth):
        z = _in_proj(h, w_in_b, l)
        merged = _mixer(z, mixer_p, l, bsz, seq)
        is_last = l == depth - 1
        nxt = (final, 0, l) if is_last else (norm1, l + 1, l + 1)
        j = l // 2
        if l % 2 == 0:
            xs, h2 = _out_proj(merged, w_o_b, xs, mod, norm2, l, seq)
            xs, h = _ffn(h2, wg_d, wu_d, wd_d, j, xs, mod, l, *nxt, seq, last=is_last)
        else:
            xs, h2, route = _out_proj(merged, w_o_b, xs, mod, norm2, l, seq, router=(w_r, b_r, j))
            outs = _moe(h2, route, wg_e, wu_e, wd_e, j * n_experts, n_experts, xs, mod, l, *nxt, seq,
                        is_last)
            if is_last:
                (h,) = outs
            else:
                xs, h = outs
    return h.reshape(bsz, seq, d)
```

```python
import functools

import jax
import jax.numpy as jnp
from jax import lax
from jax.experimental import pallas as pl
from jax.experimental.pallas import tpu as pltpu

F32 = jnp.float32
BF16 = jnp.bfloat16

V7X_LANES = 128
V7X_SUBLANES = 8
V7X_MXU_DIM = 256
V7X_VMEM_BYTES = 64 * 1024 * 1024

RG_C = 8.0
RG_CONV = 4
SC_WIDTH = 3
N_MOD = 6
EPS = 1e-6
ROUTER_PAD_BIAS = -1e30


def _params(semantics, vmem_mib):
    assert vmem_mib * 1024 * 1024 <= V7X_VMEM_BYTES
    return pltpu.CompilerParams(dimension_semantics=semantics,
                                vmem_limit_bytes=vmem_mib * 1024 * 1024)


def _dot(a, b):
    return jnp.dot(a, b, preferred_element_type=F32)


def _rms_mod(x, g, shift, scale):
    ms = jnp.mean(x * x, axis=-1, keepdims=True)
    y = x * lax.rsqrt(ms + EPS) * g
    return y * (1.0 + scale) + shift


def _mod_kernel(c_ref, w_ref, b_ref, o_ref):
    c = c_ref[...]
    c_act = (c * jax.nn.sigmoid(c)).astype(BF16)
    o_ref[...] = _dot(c_act, w_ref[...].astype(BF16)) + b_ref[...]


def _modulation(c, w_mod, b_mod, tn=1024):
    depth, d, n = w_mod.shape
    bsz = c.shape[0]
    return pl.pallas_call(
        _mod_kernel,
        grid=(depth, n // tn),
        in_specs=[
            pl.BlockSpec((bsz, d), lambda l, j: (0, 0)),
            pl.BlockSpec((None, d, tn), lambda l, j: (l, 0, j)),
            pl.BlockSpec((None, 1, tn), lambda l, j: (l, 0, j)),
        ],
        out_specs=pl.BlockSpec((None, bsz, tn), lambda l, j: (l, 0, j)),
        out_shape=jax.ShapeDtypeStruct((depth, bsz, n), F32),
        compiler_params=_params(("arbitrary", "arbitrary"), 40),
        name="modulation",
    )(c, w_mod, b_mod.reshape(depth, 1, n))


def _norm_kernel(x_ref, mod_ref, g_ref, o_ref):
    m = mod_ref[...]
    o_ref[...] = _rms_mod(x_ref[...], g_ref[...], m[0:1], m[1:2]).astype(o_ref.dtype)


def _first_norm(x, mod, norm_g, seq, tm=512):
    n, d = x.shape
    tiles_per_seq = seq // tm
    return pl.pallas_call(
        _norm_kernel,
        grid=(n // tm,),
        in_specs=[
            pl.BlockSpec((tm, d), lambda i: (i, 0)),
            pl.BlockSpec((None, None, N_MOD, d), lambda i: (0, i // tiles_per_seq, 0, 0)),
            pl.BlockSpec((None, 1, d), lambda i: (0, 0, 0)),
        ],
        out_specs=pl.BlockSpec((tm, d), lambda i: (i, 0)),
        out_shape=jax.ShapeDtypeStruct((n, d), BF16),
        compiler_params=_params(("arbitrary",), 32),
        name="first_norm",
    )(x, mod, norm_g)


def _in_proj_kernel(h_ref, w_ref, o_ref, wb):
    @pl.when(pl.program_id(1) == 0)
    def _():
        wb[...] = w_ref[...].astype(BF16)

    o_ref[...] = _dot(h_ref[...], wb[...]).astype(o_ref.dtype)


def _in_proj(h, w_in, layer, tm=1024, tn=1024):
    n, d = h.shape
    d_in = w_in.shape[-1]
    return pl.pallas_call(
        _in_proj_kernel,
        grid=(d_in // tn, n // tm),
        in_specs=[
            pl.BlockSpec((tm, d), lambda j, i: (i, 0)),
            pl.BlockSpec((None, d, tn), lambda j, i: (layer, 0, j)),
        ],
        out_specs=pl.BlockSpec((tm, tn), lambda j, i: (i, j)),
        out_shape=jax.ShapeDtypeStruct((n, d_in), BF16),
        scratch_shapes=[pltpu.VMEM((d, tn), BF16)],
        compiler_params=_params(("arbitrary", "arbitrary"), 48),
        name="in_proj",
    )(h, w_in)


def _mixer_kernel(ax_ref, ag_ref, bv_ref, bbg_ref, bcg_ref, ga0_ref, ga1_ref, gb0_ref, gb1_ref,
                  caw_ref, cab_ref, wra_ref, bra_ref, wrx_ref, brx_ref, lam_ref, woa_ref,
                  cbw_ref, cbb_ref, wob_ref, o_ref,
                  xbuf, cbuf, a_s, h_s, hcar):
    ts, w = ax_ref.shape
    pad = V7X_SUBLANES
    groups = w // V7X_MXU_DIM

    @pl.when(pl.program_id(1) == 0)
    def _():
        xbuf[0:pad, :] = jnp.zeros((pad, w), F32)
        cbuf[0:pad, :] = jnp.zeros((pad, w), F32)
        hcar[...] = jnp.zeros_like(hcar)

    xbuf[pad:pad + ts, :] = ax_ref[...].astype(F32)
    xc = cab_ref[...]
    for j in range(RG_CONV):
        xc = xc + xbuf[pl.ds(pad - (RG_CONV - 1) + j, ts), :] * caw_ref[j:j + 1, :]
    xbuf[0:pad, :] = xbuf[ts:ts + pad, :]

    xcb = xc.astype(BF16)

    def gate(w_ref, b_ref):
        parts = [_dot(xcb[:, g * V7X_MXU_DIM:(g + 1) * V7X_MXU_DIM], w_ref[g])
                 for g in range(groups)]
        return jax.nn.sigmoid(jnp.concatenate(parts, axis=-1) + b_ref[...])

    r = gate(wra_ref, bra_ref)
    i = gate(wrx_ref, brx_ref)
    nl = -lam_ref[...]
    softplus = jnp.maximum(nl, 0.0) + jnp.log1p(jnp.exp(-jnp.abs(nl)))
    log_a = (-RG_C) * r * softplus
    a = jnp.exp(log_a)
    u = jnp.sqrt(-jnp.tanh(log_a) * (a * a + 1.0)) * (i * xc)
    a_s[...] = a
    h_s[...] = u

    row = lax.broadcasted_iota(jnp.int32, (V7X_SUBLANES, w), 0)

    def scan_group(k, carry):
        off = pl.multiple_of(k * V7X_SUBLANES, V7X_SUBLANES)
        ca = a_s[pl.ds(off, V7X_SUBLANES), :]
        cb = h_s[pl.ds(off, V7X_SUBLANES), :]
        d = 1
        while d < V7X_SUBLANES:
            keep = row >= d
            a_prev = jnp.where(keep, pltpu.roll(ca, d, 0), 1.0)
            b_prev = jnp.where(keep, pltpu.roll(cb, d, 0), 0.0)
            cb = ca * b_prev + cb
            ca = ca * a_prev
            d *= 2
        hh = ca * carry + cb
        h_s[pl.ds(off, V7X_SUBLANES), :] = hh
        return jnp.broadcast_to(hh[V7X_SUBLANES - 1:V7X_SUBLANES, :], (V7X_SUBLANES, w))

    hcar[...] = lax.fori_loop(0, ts // V7X_SUBLANES, scan_group, hcar[...], unroll=2)

    pa = (jax.nn.gelu(ag_ref[...].astype(F32), approximate=True) * h_s[...]).astype(BF16)
    ya = _dot(pa, woa_ref[...])

    cbuf[pad:pad + ts, :] = bcg_ref[...].astype(F32) * bv_ref[...].astype(F32)
    cv = cbb_ref[...]
    for j in range(SC_WIDTH):
        cv = cv + cbuf[pl.ds(pad - (SC_WIDTH - 1) + j, ts), :] * cbw_ref[j:j + 1, :]
    cbuf[0:pad, :] = cbuf[ts:ts + pad, :]
    pb = (bbg_ref[...].astype(F32) * cv).astype(BF16)
    yb = _dot(pb, wob_ref[...])

    for half, (ga_ref, gb_ref) in enumerate(((ga0_ref, gb0_ref), (ga1_ref, gb1_ref))):
        cols = slice(half * w, (half + 1) * w)
        o_ref[:, cols] = (jax.nn.sigmoid(ga_ref[...].astype(F32)) * ya[:, cols]
                          + jax.nn.sigmoid(gb_ref[...].astype(F32)) * yb[:, cols]).astype(o_ref.dtype)


def _mixer(z, p, layer, bsz, seq, ts=256):
    n = z.shape[0]
    w = p["conv_a_w"].shape[-1]
    d = p["w_out_a"].shape[-1]
    assert p["conv_b_w"].shape[-1] == w and d == 2 * w and w % V7X_MXU_DIM == 0
    tiles = seq // ts

    def zcol(c):
        return pl.BlockSpec((ts, w), lambda b, s: (b * tiles + s, c))

    def vec():
        return pl.BlockSpec((None, 1, w), lambda b, s: (layer, 0, 0))

    def full(shape):
        nd = len(shape)
        return pl.BlockSpec((None,) + shape, lambda b, s: (layer,) + (0,) * nd)

    groups = w // V7X_MXU_DIM
    return pl.pallas_call(
        _mixer_kernel,
        grid=(bsz, tiles),
        in_specs=[zcol(c) for c in range(9)] + [
            full((RG_CONV, w)), vec(),
            full((groups, V7X_MXU_DIM, V7X_MXU_DIM)), vec(),
            full((groups, V7X_MXU_DIM, V7X_MXU_DIM)), vec(),
            vec(), full((w, d)),
            full((SC_WIDTH, w)), vec(), full((w, d)),
        ],
        out_specs=pl.BlockSpec((ts, d), lambda b, s: (b * tiles + s, 0)),
        out_shape=jax.ShapeDtypeStruct((n, d), BF16),
        scratch_shapes=[
            pltpu.VMEM((ts + 2 * V7X_SUBLANES, w), F32),
            pltpu.VMEM((ts + 2 * V7X_SUBLANES, w), F32),
            pltpu.VMEM((ts, w), F32),
            pltpu.VMEM((ts, w), F32),
            pltpu.VMEM((V7X_SUBLANES, w), F32),
        ],
        compiler_params=_params(("arbitrary", "arbitrary"), 56),
        name="mixer",
    )(*([z] * 9), p["conv_a_w"], p["conv_a_b"], p["w_rg_a"], p["b_rg_a"], p["w_rg_x"], p["b_rg_x"],
      p["rg_lambda"], p["w_out_a"], p["conv_b_w"], p["conv_b_b"], p["w_out_b"])


ROUTE_FIRST, ROUTE_SECOND, ROUTE_W_FIRST, ROUTE_W_SECOND = 0, 1, 2, 3


def _top2_route(logits):
    lane = lax.broadcasted_iota(jnp.int32, logits.shape, 1).astype(F32)
    n_lanes = float(logits.shape[-1])
    v1 = jnp.max(logits, axis=-1, keepdims=True)
    i1 = jnp.min(jnp.where(logits == v1, lane, n_lanes), axis=-1, keepdims=True)
    rest = jnp.where(lane == i1, -jnp.inf, logits)
    v2 = jnp.max(rest, axis=-1, keepdims=True)
    i2 = jnp.min(jnp.where(rest == v2, lane, n_lanes), axis=-1, keepdims=True)
    e = jnp.exp(v2 - v1)
    w1 = 1.0 / (1.0 + e)
    w2 = e / (1.0 + e)
    out = jnp.where(lane == ROUTE_FIRST, i1, 0.0)
    out = jnp.where(lane == ROUTE_SECOND, i2, out)
    out = jnp.where(lane == ROUTE_W_FIRST, w1, out)
    return jnp.where(lane == ROUTE_W_SECOND, w2, out)


def _out_proj_kernel(*refs, with_router):
    if with_router:
        m_ref, wo_ref, x_ref, mod_ref, g_ref, wr_ref, br_ref, xo_ref, ho_ref, route_ref = refs
    else:
        m_ref, wo_ref, x_ref, mod_ref, g_ref, xo_ref, ho_ref = refs
    mod = mod_ref[...]
    xn = x_ref[...] + mod[2:3] * _dot(m_ref[...], wo_ref[...])
    xo_ref[...] = xn
    h2 = _rms_mod(xn, g_ref[...], mod[3:4], mod[4:5])
    ho_ref[...] = h2.astype(ho_ref.dtype)
    if with_router:
        route_ref[...] = _top2_route(_dot(h2.astype(BF16), wr_ref[...]) + br_ref[...])


def _out_proj(merged, w_o, x, mod, norm_g, layer, seq, router=None, tm=512):
    n, d = x.shape
    tiles_per_seq = seq // tm
    in_specs = [
        pl.BlockSpec((tm, d), lambda i: (i, 0)),
        pl.BlockSpec((None, d, d), lambda i: (layer, 0, 0)),
        pl.BlockSpec((tm, d), lambda i: (i, 0)),
        pl.BlockSpec((None, None, N_MOD, d), lambda i: (layer, i // tiles_per_seq, 0, 0)),
        pl.BlockSpec((None, 1, d), lambda i: (layer, 0, 0)),
    ]
    out_specs = [pl.BlockSpec((tm, d), lambda i: (i, 0)), pl.BlockSpec((tm, d), lambda i: (i, 0))]
    out_shape = [jax.ShapeDtypeStruct((n, d), F32),
                 jax.ShapeDtypeStruct((n, d), BF16 if router is None else F32)]
    args = [merged, w_o, x, mod, norm_g]
    if router is not None:
        w_r, b_r, j = router
        in_specs += [pl.BlockSpec((None, d, V7X_LANES), lambda i: (j, 0, 0)),
                     pl.BlockSpec((None, 1, V7X_LANES), lambda i: (j, 0, 0))]
        out_specs.append(pl.BlockSpec((tm, V7X_LANES), lambda i: (i, 0)))
        out_shape.append(jax.ShapeDtypeStruct((n, V7X_LANES), F32))
        args += [w_r, b_r]
    return pl.pallas_call(
        functools.partial(_out_proj_kernel, with_router=router is not None),
        grid=(n // tm,),
        in_specs=in_specs, out_specs=out_specs, out_shape=out_shape,
        input_output_aliases={2: 0},
        compiler_params=_params(("arbitrary",), 56),
        name="out_proj",
    )(*args)


def _residual_norm(x, gate, f, g, nmod, last):
    xn = x + gate * f
    if last:
        zero = jnp.zeros_like(g)
        return xn, _rms_mod(xn, g, zero, zero)
    return xn, _rms_mod(xn, g, nmod[0:1], nmod[1:2])


def _swiglu_accumulate(x_ref, wg_ref, wu_ref, wd_ref, acc_ref):
    h = x_ref[...]
    a = _dot(h, wg_ref[...])
    act = (a * jax.nn.sigmoid(a)) * _dot(h, wu_ref[...])
    acc_ref[...] += _dot(act.astype(BF16), wd_ref[...])


def _ffn_kernel(h_ref, wg_ref, wu_ref, wd_ref, x_ref, mod_ref, g_ref, nmod_ref, xo_ref, ho_ref, *, last):
    j = pl.program_id(1)

    @pl.when(j == 0)
    def _():
        xo_ref[...] = jnp.zeros_like(xo_ref)

    _swiglu_accumulate(h_ref, wg_ref, wu_ref, wd_ref, xo_ref)

    @pl.when(j == pl.num_programs(1) - 1)
    def _():
        xn, hn = _residual_norm(x_ref[...], mod_ref[...][5:6], xo_ref[...], g_ref[...], nmod_ref[...], last)
        xo_ref[...] = xn
        ho_ref[...] = hn.astype(ho_ref.dtype)


def _ffn(h, wg, wu, wd, w_index, x, mod, layer, next_g, next_g_index, next_layer, seq,
         last=False, tm=512, tf=1024):
    n, d = x.shape
    f = wg.shape[-1]
    tiles_per_seq = seq // tm
    return pl.pallas_call(
        functools.partial(_ffn_kernel, last=last),
        grid=(n // tm, f // tf),
        in_specs=[
            pl.BlockSpec((tm, d), lambda i, j: (i, 0)),
            pl.BlockSpec((None, d, tf), lambda i, j: (w_index, 0, j)),
            pl.BlockSpec((None, d, tf), lambda i, j: (w_index, 0, j)),
            pl.BlockSpec((None, tf, d), lambda i, j: (w_index, j, 0)),
            pl.BlockSpec((tm, d), lambda i, j: (i, 0)),
            pl.BlockSpec((None, None, N_MOD, d), lambda i, j: (layer, i // tiles_per_seq, 0, 0)),
            pl.BlockSpec((None, 1, d), lambda i, j: (next_g_index, 0, 0)),
            pl.BlockSpec((None, None, N_MOD, d), lambda i, j: (next_layer, i // tiles_per_seq, 0, 0)),
        ],
        out_specs=[pl.BlockSpec((tm, d), lambda i, j: (i, 0)),
                   pl.BlockSpec((tm, d), lambda i, j: (i, 0))],
        out_shape=[jax.ShapeDtypeStruct((n, d), F32),
                   jax.ShapeDtypeStruct((n, d), F32 if last else BF16)],
        input_output_aliases={4: 0},
        compiler_params=_params(("arbitrary", "arbitrary"), 58),
        name="ffn",
    )(h, wg, wu, wd, x, mod, next_g, mod)


TOP_K = 2
ROW_DMA_UNROLL = 8


def _route_plan(route, n_experts, tm):
    n = route.shape[0]
    n_tiles = TOP_K * n // tm + n_experts - 1
    choice = route[:, ROUTE_FIRST:ROUTE_SECOND + 1].astype(jnp.int32)
    member = jnp.sum(choice[:, :, None] == jnp.arange(n_experts)[None, None, :], axis=1,
                     dtype=jnp.int32)
    rank = jnp.cumsum(member, axis=0) - member
    counts = rank[-1] + member[-1]
    padded = (counts + tm - 1) // tm * tm
    ends = jnp.cumsum(padded)
    dest = jnp.take_along_axis((ends - padded)[None, :] + rank, choice, axis=1)
    n_used = ends[-1] // tm
    tile = jnp.minimum(jnp.arange(n_tiles), n_used - 1)
    tile_expert = jnp.sum(tile[:, None] * tm >= ends[None, :], axis=1, dtype=jnp.int32)
    return dest.reshape(-1).astype(jnp.int32), tile_expert, n_used.reshape(1).astype(jnp.int32), n_tiles


def _dispatch_kernel(dest_ref, h_ref, init_ref, o_ref, sem):
    del init_ref
    tt = h_ref.shape[0]
    base = pl.program_id(0) * tt

    def row_copy(r, k):
        row = dest_ref[TOP_K * (base + r) + k]
        return pltpu.make_async_copy(h_ref.at[pl.ds(r, 1), :], o_ref.at[pl.ds(row, 1), :], sem)

    @pl.loop(0, tt, unroll=ROW_DMA_UNROLL)
    def _(r):
        for k in range(TOP_K):
            row_copy(r, k).start(priority=k)

    @pl.loop(0, tt, unroll=ROW_DMA_UNROLL)
    def _(r):
        for k in range(TOP_K):
            row_copy(r, k).wait()


def _dispatch(h, dest, n_rows, tt=256):
    n, d = h.shape
    return pl.pallas_call(
        _dispatch_kernel,
        grid_spec=pltpu.PrefetchScalarGridSpec(
            num_scalar_prefetch=1,
            grid=(n // tt,),
            in_specs=[pl.BlockSpec((tt, d), lambda i, dest: (i, 0)),
                      pl.BlockSpec(memory_space=pl.ANY)],
            out_specs=pl.BlockSpec(memory_space=pl.ANY),
            scratch_shapes=[pltpu.SemaphoreType.DMA(())],
        ),
        out_shape=jax.ShapeDtypeStruct((n_rows, d), h.dtype),
        input_output_aliases={2: 0},
        compiler_params=_params(("arbitrary",), 32),
        name="dispatch",
    )(dest, h, jnp.zeros((n_rows, d), h.dtype))


def _expert_kernel(te_ref, nu_ref, x_ref, wg_ref, wu_ref, wd_ref, y_ref, xb):
    del te_ref
    j = pl.program_id(1)
    used = pl.program_id(0) < nu_ref[0]

    @pl.when(jnp.logical_and(jnp.logical_not(used), j == 0))
    def _():
        y_ref[...] = jnp.zeros_like(y_ref)

    @pl.when(used)
    def _():
        @pl.when(j == 0)
        def _():
            xb[...] = x_ref[...].astype(BF16)
            y_ref[...] = jnp.zeros_like(y_ref)

        _swiglu_accumulate(xb, wg_ref, wu_ref, wd_ref, y_ref)


def _expert_ffn(xs, wg, wu, wd, w_base, tile_expert, n_used, n_tiles, tm, tf=1024):
    d = xs.shape[-1]
    f = wg.shape[-1]
    n_j = f // tf

    def row(i, j, te, nu):
        return (jnp.minimum(i, nu[0] - 1), 0)

    def col(i, j, nu):
        return jnp.where(i < nu[0], j, n_j - 1)

    return pl.pallas_call(
        _expert_kernel,
        grid_spec=pltpu.PrefetchScalarGridSpec(
            num_scalar_prefetch=2,
            grid=(n_tiles, n_j),
            in_specs=[
                pl.BlockSpec((tm, d), row),
                pl.BlockSpec((None, d, tf), lambda i, j, te, nu: (w_base + te[i], 0, col(i, j, nu))),
                pl.BlockSpec((None, d, tf), lambda i, j, te, nu: (w_base + te[i], 0, col(i, j, nu))),
                pl.BlockSpec((None, tf, d), lambda i, j, te, nu: (w_base + te[i], col(i, j, nu), 0)),
            ],
            out_specs=pl.BlockSpec((tm, d), lambda i, j, te, nu: (i, 0)),
            scratch_shapes=[pltpu.VMEM((tm, d), BF16)],
        ),
        out_shape=jax.ShapeDtypeStruct((n_tiles * tm, d), F32),
        compiler_params=_params(("arbitrary", "arbitrary"), 56),
        name="expert_ffn",
    )(tile_expert, n_used, xs, wg, wu, wd)


def _combine_kernel(dest_ref, y_ref, x_ref, route_ref, mod_ref, g_ref, nmod_ref, *rest, last):
    if last:
        ho_ref, ybuf, sem = rest
    else:
        xo_ref, ho_ref, ybuf, sem = rest
    tc = x_ref.shape[0]
    base = pl.program_id(0) * tc

    def row_copy(r, k):
        row = dest_ref[TOP_K * (base + r) + k]
        return pltpu.make_async_copy(y_ref.at[pl.ds(row, 1), :], ybuf.at[k, pl.ds(r, 1), :], sem.at[k])

    @pl.loop(0, tc, unroll=ROW_DMA_UNROLL)
    def _(r):
        for k in range(TOP_K):
            row_copy(r, k).start(priority=k)

    @pl.loop(0, tc, unroll=ROW_DMA_UNROLL)
    def _(r):
        for k in range(TOP_K):
            row_copy(r, k).wait()

    route = route_ref[...]
    f = (route[:, ROUTE_W_FIRST:ROUTE_W_FIRST + 1] * ybuf[0]
         + route[:, ROUTE_W_SECOND:ROUTE_W_SECOND + 1] * ybuf[1])
    xn, hn = _residual_norm(x_ref[...], mod_ref[...][5:6], f, g_ref[...], nmod_ref[...], last)
    if not last:
        xo_ref[...] = xn
    ho_ref[...] = hn.astype(ho_ref.dtype)


def _combine(y, dest, x, route, mod, layer, next_g, next_g_index, next_layer, seq, last, tc=256):
    n, d = x.shape
    tiles_per_seq = seq // tc
    tok = pl.BlockSpec((tc, d), lambda i, dest: (i, 0))
    out_specs = [tok] if last else [tok, tok]
    out_shape = ([jax.ShapeDtypeStruct((n, d), F32)] if last else
                 [jax.ShapeDtypeStruct((n, d), F32), jax.ShapeDtypeStruct((n, d), BF16)])
    return pl.pallas_call(
        functools.partial(_combine_kernel, last=last),
        grid_spec=pltpu.PrefetchScalarGridSpec(
            num_scalar_prefetch=1,
            grid=(n // tc,),
            in_specs=[
                pl.BlockSpec(memory_space=pl.ANY),
                tok,
                pl.BlockSpec((tc, V7X_LANES), lambda i, dest: (i, 0)),
                pl.BlockSpec((None, None, N_MOD, d), lambda i, dest: (layer, i // tiles_per_seq, 0, 0)),
                pl.BlockSpec((None, 1, d), lambda i, dest: (next_g_index, 0, 0)),
                pl.BlockSpec((None, None, N_MOD, d), lambda i, dest: (next_layer, i // tiles_per_seq, 0, 0)),
            ],
            out_specs=out_specs,
            scratch_shapes=[pltpu.VMEM((TOP_K, tc, d), F32), pltpu.SemaphoreType.DMA((TOP_K,))],
        ),
        out_shape=out_shape,
        input_output_aliases={} if last else {2: 0},
        compiler_params=_params(("arbitrary",), 48),
        name="combine",
    )(dest, y, x, route, mod, next_g, mod)


def _moe(h2, route, wg, wu, wd, w_base, n_experts, x, mod, layer, next_g, next_g_index, next_layer,
         seq, last, tm=512):
    dest, tile_expert, n_used, n_tiles = _route_plan(route, n_experts, tm)
    xs = _dispatch(h2, dest, n_tiles * tm)
    y = _expert_ffn(xs, wg, wu, wd, w_base, tile_expert, n_used, n_tiles, tm)
    return _combine(y, dest, x, route, mod, layer, next_g, next_g_index, next_layer, seq, last)


def _block_diag_groups(w_heads):
    depth, heads, hd, _ = w_heads.shape
    per = V7X_MXU_DIM // hd
    groups = heads // per
    wg = w_heads.reshape(depth, groups, per, hd, hd)
    eye = jnp.eye(per, dtype=w_heads.dtype)
    out = jnp.einsum("lgpde,pq->lgpdqe", wg, eye)
    return out.reshape(depth, groups, V7X_MXU_DIM, V7X_MXU_DIM)


def kernel(x, c, w_mod, b_mod, norm1_g, norm2_g, w_in, conv_a_w, conv_a_b, w_rg_a, b_rg_a, w_rg_x, b_rg_x, rg_lambda, w_out_a, conv_b_w, conv_b_b, w_out_b, w_o, w_ff_gate, w_ff_up, w_ff_down, w_router, b_router, w_e_gate, w_e_up, w_e_down, final_g):
    bsz, seq, d = x.shape
    depth = w_in.shape[0]
    n = bsz * seq
    n_moe, n_experts = w_router.shape[0], w_router.shape[-1]
    w_rnn = conv_a_w.shape[-1]

    w_o_b = w_o.astype(BF16)
    mixer_p = dict(
        conv_a_w=conv_a_w, conv_a_b=conv_a_b.reshape(depth, 1, w_rnn),
        w_rg_a=_block_diag_groups(w_rg_a).astype(BF16), b_rg_a=b_rg_a.reshape(depth, 1, w_rnn),
        w_rg_x=_block_diag_groups(w_rg_x).astype(BF16), b_rg_x=b_rg_x.reshape(depth, 1, w_rnn),
        rg_lambda=rg_lambda.reshape(depth, 1, w_rnn), w_out_a=w_out_a.astype(BF16),
        conv_b_w=conv_b_w, conv_b_b=conv_b_b.reshape(depth, 1, w_rnn), w_out_b=w_out_b.astype(BF16),
    )
    wg_d, wu_d, wd_d = (w.astype(BF16) for w in (w_ff_gate, w_ff_up, w_ff_down))
    f_e = w_e_gate.shape[-1]
    wg_e = w_e_gate.astype(BF16).reshape(n_moe * n_experts, d, f_e)
    wu_e = w_e_up.astype(BF16).reshape(n_moe * n_experts, d, f_e)
    wd_e = w_e_down.astype(BF16).reshape(n_moe * n_experts, f_e, d)
    w_r = jnp.pad(w_router, ((0, 0), (0, 0), (0, V7X_LANES - n_experts))).astype(BF16)
    b_r = jnp.pad(b_router, ((0, 0), (0, V7X_LANES - n_experts)),
                  constant_values=ROUTER_PAD_BIAS).reshape(n_moe, 1, V7X_LANES)
    norm1 = norm1_g.reshape(depth, 1, d)
    norm2 = norm2_g.reshape(depth, 1, d)
    final = final_g.reshape(1, 1, d)

    mod = _modulation(c, w_mod, b_mod).reshape(depth, bsz, N_MOD, d)
    xs = x.reshape(n, d)
    h = _first_norm(xs, mod, norm1, seq)
    for l in range(depth):
        z = _in_proj(h, w_in, l)
        merged = _mixer(z, mixer_p, l, bsz, seq)
        is_last = l == depth - 1
        nxt = (final, 0, l) if is_last else (norm1, l + 1, l + 1)
        j = l // 2
        if l % 2 == 0:
            xs, h2 = _out_proj(merged, w_o_b, xs, mod, norm2, l, seq)
            xs, h = _ffn(h2, wg_d, wu_d, wd_d, j, xs, mod, l, *nxt, seq, last=is_last)
        else:
            xs, h2, route = _out_proj(merged, w_o_b, xs, mod, norm2, l, seq, router=(w_r, b_r, j))
            outs = _moe(h2, route, wg_e, wu_e, wd_e, j * n_experts, n_experts, xs, mod, l, *nxt, seq,
                        is_last)
            if is_last:
                (h,) = outs
            else:
                xs, h = outs
    return h.reshape(bsz, seq, d)
```

```python
import functools

import jax
import jax.numpy as jnp
from jax import lax
from jax.experimental import pallas as pl
from jax.experimental.pallas import tpu as pltpu

F32 = jnp.float32
BF16 = jnp.bfloat16

V7X_LANES = 128
V7X_SUBLANES = 8
V7X_MXU_DIM = 256
V7X_VMEM_BYTES = 64 * 1024 * 1024

RG_C = 8.0
RG_CONV = 4
SC_WIDTH = 3
N_MOD = 6
EPS = 1e-6
ROUTER_PAD_BIAS = -1e30


def _params(semantics, vmem_mib):
    assert vmem_mib * 1024 * 1024 <= V7X_VMEM_BYTES
    return pltpu.CompilerParams(dimension_semantics=semantics,
                                vmem_limit_bytes=vmem_mib * 1024 * 1024)


def _dot(a, b):
    return jnp.dot(a, b, preferred_element_type=F32)


def _rms_mod(x, g, shift, scale):
    ms = jnp.mean(x * x, axis=-1, keepdims=True)
    y = x * lax.rsqrt(ms + EPS) * g
    return y * (1.0 + scale) + shift


def _mod_kernel(c_ref, w_ref, b_ref, o_ref):
    c = c_ref[...]
    c_act = (c * jax.nn.sigmoid(c)).astype(BF16)
    o_ref[...] = _dot(c_act, w_ref[...].astype(BF16)) + b_ref[...]


def _modulation(c, w_mod, b_mod, tn=1024):
    depth, d, n = w_mod.shape
    bsz = c.shape[0]
    return pl.pallas_call(
        _mod_kernel,
        grid=(depth, n // tn),
        in_specs=[
            pl.BlockSpec((bsz, d), lambda l, j: (0, 0)),
            pl.BlockSpec((None, d, tn), lambda l, j: (l, 0, j)),
            pl.BlockSpec((None, 1, tn), lambda l, j: (l, 0, j)),
        ],
        out_specs=pl.BlockSpec((None, bsz, tn), lambda l, j: (l, 0, j)),
        out_shape=jax.ShapeDtypeStruct((depth, bsz, n), F32),
        compiler_params=_params(("arbitrary", "arbitrary"), 40),
        name="modulation",
    )(c, w_mod, b_mod.reshape(depth, 1, n))


def _norm_kernel(x_ref, mod_ref, g_ref, o_ref):
    m = mod_ref[...]
    o_ref[...] = _rms_mod(x_ref[...], g_ref[...], m[0:1], m[1:2]).astype(o_ref.dtype)


def _first_norm(x, mod, norm_g, seq, tm=512):
    n, d = x.shape
    tiles_per_seq = seq // tm
    return pl.pallas_call(
        _norm_kernel,
        grid=(n // tm,),
        in_specs=[
            pl.BlockSpec((tm, d), lambda i: (i, 0)),
            pl.BlockSpec((None, None, N_MOD, d), lambda i: (0, i // tiles_per_seq, 0, 0)),
            pl.BlockSpec((None, 1, d), lambda i: (0, 0, 0)),
        ],
        out_specs=pl.BlockSpec((tm, d), lambda i: (i, 0)),
        out_shape=jax.ShapeDtypeStruct((n, d), BF16),
        compiler_params=_params(("arbitrary",), 32),
        name="first_norm",
    )(x, mod, norm_g)


ROUND_CHUNKS = 128


def _in_proj_kernel(h_ref, w_ref, *rest, n_side):
    side_in, o_ref, side_out, wb = rest[:n_side], rest[n_side], rest[n_side + 1:-1], rest[-1]

    @pl.when(pl.program_id(1) == 0)
    def _():
        wb[...] = w_ref[...].astype(BF16)

    o_ref[...] = _dot(h_ref[...], wb[...]).astype(o_ref.dtype)

    for src, dst in zip(side_in, side_out):
        dst[...] = src[...].astype(BF16)


def _in_proj(h, w_in, layer, ffn_weights, ffn_index, tm=1024, tn=1024):
    n, d = h.shape
    d_in = w_in.shape[-1]
    grid = (d_in // tn, n // tm)
    assert grid[0] * grid[1] >= ROUND_CHUNKS

    def chunk(j, i):
        return jnp.minimum(j * grid[1] + i, ROUND_CHUNKS - 1)

    side_in, side_out, side_shape = [], [], []
    for w in ffn_weights:
        _, rows, cols = w.shape
        side_in.append(pl.BlockSpec((None, rows // ROUND_CHUNKS, cols),
                                    lambda j, i: (ffn_index, chunk(j, i), 0)))
        side_out.append(pl.BlockSpec((rows // ROUND_CHUNKS, cols), lambda j, i: (chunk(j, i), 0)))
        side_shape.append(jax.ShapeDtypeStruct((rows, cols), BF16))
    return pl.pallas_call(
        functools.partial(_in_proj_kernel, n_side=len(ffn_weights)),
        grid=grid,
        in_specs=[
            pl.BlockSpec((tm, d), lambda j, i: (i, 0)),
            pl.BlockSpec((None, d, tn), lambda j, i: (layer, 0, j)),
        ] + side_in,
        out_specs=[pl.BlockSpec((tm, tn), lambda j, i: (i, j))] + side_out,
        out_shape=[jax.ShapeDtypeStruct((n, d_in), BF16)] + side_shape,
        scratch_shapes=[pltpu.VMEM((d, tn), BF16)],
        compiler_params=_params(("arbitrary", "arbitrary"), 56),
        name="in_proj",
    )(h, w_in, *ffn_weights)


def _mixer_kernel(ax_ref, ag_ref, bv_ref, bbg_ref, bcg_ref, ga0_ref, ga1_ref, gb0_ref, gb1_ref,
                  caw_ref, cab_ref, wra_ref, bra_ref, wrx_ref, brx_ref, lam_ref, woa_ref,
                  cbw_ref, cbb_ref, wob_ref, o_ref,
                  xbuf, cbuf, a_s, h_s, hcar):
    ts, w = ax_ref.shape
    pad = V7X_SUBLANES
    groups = w // V7X_MXU_DIM

    @pl.when(pl.program_id(1) == 0)
    def _():
        xbuf[0:pad, :] = jnp.zeros((pad, w), F32)
        cbuf[0:pad, :] = jnp.zeros((pad, w), F32)
        hcar[...] = jnp.zeros_like(hcar)

    xbuf[pad:pad + ts, :] = ax_ref[...].astype(F32)
    xc = cab_ref[...]
    for j in range(RG_CONV):
        xc = xc + xbuf[pl.ds(pad - (RG_CONV - 1) + j, ts), :] * caw_ref[j:j + 1, :]
    xbuf[0:pad, :] = xbuf[ts:ts + pad, :]

    xcb = xc.astype(BF16)

    def gate(w_ref, b_ref):
        parts = [_dot(xcb[:, g * V7X_MXU_DIM:(g + 1) * V7X_MXU_DIM], w_ref[g])
                 for g in range(groups)]
        return jax.nn.sigmoid(jnp.concatenate(parts, axis=-1) + b_ref[...])

    r = gate(wra_ref, bra_ref)
    i = gate(wrx_ref, brx_ref)
    nl = -lam_ref[...]
    softplus = jnp.maximum(nl, 0.0) + jnp.log1p(jnp.exp(-jnp.abs(nl)))
    log_a = (-RG_C) * r * softplus
    a = jnp.exp(log_a)
    u = jnp.sqrt(-jnp.tanh(log_a) * (a * a + 1.0)) * (i * xc)
    a_s[...] = a
    h_s[...] = u

    row = lax.broadcasted_iota(jnp.int32, (V7X_SUBLANES, w), 0)

    def scan_group(k, carry):
        off = pl.multiple_of(k * V7X_SUBLANES, V7X_SUBLANES)
        ca = a_s[pl.ds(off, V7X_SUBLANES), :]
        cb = h_s[pl.ds(off, V7X_SUBLANES), :]
        d = 1
        while d < V7X_SUBLANES:
            keep = row >= d
            a_prev = jnp.where(keep, pltpu.roll(ca, d, 0), 1.0)
            b_prev = jnp.where(keep, pltpu.roll(cb, d, 0), 0.0)
            cb = ca * b_prev + cb
            ca = ca * a_prev
            d *= 2
        hh = ca * carry + cb
        h_s[pl.ds(off, V7X_SUBLANES), :] = hh
        return jnp.broadcast_to(hh[V7X_SUBLANES - 1:V7X_SUBLANES, :], (V7X_SUBLANES, w))

    hcar[...] = lax.fori_loop(0, ts // V7X_SUBLANES, scan_group, hcar[...], unroll=2)

    pa = (jax.nn.gelu(ag_ref[...].astype(F32), approximate=True) * h_s[...]).astype(BF16)
    ya = _dot(pa, woa_ref[...])

    cbuf[pad:pad + ts, :] = bcg_ref[...].astype(F32) * bv_ref[...].astype(F32)
    cv = cbb_ref[...]
    for j in range(SC_WIDTH):
        cv = cv + cbuf[pl.ds(pad - (SC_WIDTH - 1) + j, ts), :] * cbw_ref[j:j + 1, :]
    cbuf[0:pad, :] = cbuf[ts:ts + pad, :]
    pb = (bbg_ref[...].astype(F32) * cv).astype(BF16)
    yb = _dot(pb, wob_ref[...])

    for half, (ga_ref, gb_ref) in enumerate(((ga0_ref, gb0_ref), (ga1_ref, gb1_ref))):
        cols = slice(half * w, (half + 1) * w)
        o_ref[:, cols] = (jax.nn.sigmoid(ga_ref[...].astype(F32)) * ya[:, cols]
                          + jax.nn.sigmoid(gb_ref[...].astype(F32)) * yb[:, cols]).astype(o_ref.dtype)


def _mixer(z, p, layer, bsz, seq, ts=256):
    n = z.shape[0]
    w = p["conv_a_w"].shape[-1]
    d = p["w_out_a"].shape[-1]
    assert p["conv_b_w"].shape[-1] == w and d == 2 * w and w % V7X_MXU_DIM == 0
    tiles = seq // ts

    def zcol(c):
        return pl.BlockSpec((ts, w), lambda b, s: (b * tiles + s, c))

    def vec():
        return pl.BlockSpec((None, 1, w), lambda b, s: (layer, 0, 0))

    def full(shape):
        nd = len(shape)
        return pl.BlockSpec((None,) + shape, lambda b, s: (layer,) + (0,) * nd)

    groups = w // V7X_MXU_DIM
    return pl.pallas_call(
        _mixer_kernel,
        grid=(bsz, tiles),
        in_specs=[zcol(c) for c in range(9)] + [
            full((RG_CONV, w)), vec(),
            full((groups, V7X_MXU_DIM, V7X_MXU_DIM)), vec(),
            full((groups, V7X_MXU_DIM, V7X_MXU_DIM)), vec(),
            vec(), full((w, d)),
            full((SC_WIDTH, w)), vec(), full((w, d)),
        ],
        out_specs=pl.BlockSpec((ts, d), lambda b, s: (b * tiles + s, 0)),
        out_shape=jax.ShapeDtypeStruct((n, d), BF16),
        scratch_shapes=[
            pltpu.VMEM((ts + 2 * V7X_SUBLANES, w), F32),
            pltpu.VMEM((ts + 2 * V7X_SUBLANES, w), F32),
            pltpu.VMEM((ts, w), F32),
            pltpu.VMEM((ts, w), F32),
            pltpu.VMEM((V7X_SUBLANES, w), F32),
        ],
        compiler_params=_params(("arbitrary", "arbitrary"), 56),
        name="mixer",
    )(*([z] * 9), p["conv_a_w"], p["conv_a_b"], p["w_rg_a"], p["b_rg_a"], p["w_rg_x"], p["b_rg_x"],
      p["rg_lambda"], p["w_out_a"], p["conv_b_w"], p["conv_b_b"], p["w_out_b"])


ROUTE_FIRST, ROUTE_SECOND, ROUTE_W_FIRST, ROUTE_W_SECOND = 0, 1, 2, 3


def _top2_route(logits):
    lane = lax.broadcasted_iota(jnp.int32, logits.shape, 1).astype(F32)
    n_lanes = float(logits.shape[-1])
    v1 = jnp.max(logits, axis=-1, keepdims=True)
    i1 = jnp.min(jnp.where(logits == v1, lane, n_lanes), axis=-1, keepdims=True)
    rest = jnp.where(lane == i1, -jnp.inf, logits)
    v2 = jnp.max(rest, axis=-1, keepdims=True)
    i2 = jnp.min(jnp.where(rest == v2, lane, n_lanes), axis=-1, keepdims=True)
    e = jnp.exp(v2 - v1)
    w1 = 1.0 / (1.0 + e)
    w2 = e / (1.0 + e)
    out = jnp.where(lane == ROUTE_FIRST, i1, 0.0)
    out = jnp.where(lane == ROUTE_SECOND, i2, out)
    out = jnp.where(lane == ROUTE_W_FIRST, w1, out)
    return jnp.where(lane == ROUTE_W_SECOND, w2, out)


def _out_proj_kernel(*refs, with_router):
    if with_router:
        m_ref, wo_ref, x_ref, mod_ref, g_ref, wr_ref, br_ref, xo_ref, ho_ref, route_ref = refs
    else:
        m_ref, wo_ref, x_ref, mod_ref, g_ref, xo_ref, ho_ref = refs
    mod = mod_ref[...]
    xn = x_ref[...] + mod[2:3] * _dot(m_ref[...], wo_ref[...])
    xo_ref[...] = xn
    h2 = _rms_mod(xn, g_ref[...], mod[3:4], mod[4:5])
    ho_ref[...] = h2.astype(ho_ref.dtype)
    if with_router:
        route_ref[...] = _top2_route(_dot(h2.astype(BF16), wr_ref[...]) + br_ref[...])


def _out_proj(merged, w_o, x, mod, norm_g, layer, seq, router=None, tm=512):
    n, d = x.shape
    tiles_per_seq = seq // tm
    in_specs = [
        pl.BlockSpec((tm, d), lambda i: (i, 0)),
        pl.BlockSpec((None, d, d), lambda i: (layer, 0, 0)),
        pl.BlockSpec((tm, d), lambda i: (i, 0)),
        pl.BlockSpec((None, None, N_MOD, d), lambda i: (layer, i // tiles_per_seq, 0, 0)),
        pl.BlockSpec((None, 1, d), lambda i: (layer, 0, 0)),
    ]
    out_specs = [pl.BlockSpec((tm, d), lambda i: (i, 0)), pl.BlockSpec((tm, d), lambda i: (i, 0))]
    out_shape = [jax.ShapeDtypeStruct((n, d), F32),
                 jax.ShapeDtypeStruct((n, d), BF16 if router is None else F32)]
    args = [merged, w_o, x, mod, norm_g]
    if router is not None:
        w_r, b_r, j = router
        in_specs += [pl.BlockSpec((None, d, V7X_LANES), lambda i: (j, 0, 0)),
                     pl.BlockSpec((None, 1, V7X_LANES), lambda i: (j, 0, 0))]
        out_specs.append(pl.BlockSpec((tm, V7X_LANES), lambda i: (i, 0)))
        out_shape.append(jax.ShapeDtypeStruct((n, V7X_LANES), F32))
        args += [w_r, b_r]
    return pl.pallas_call(
        functools.partial(_out_proj_kernel, with_router=router is not None),
        grid=(n // tm,),
        in_specs=in_specs, out_specs=out_specs, out_shape=out_shape,
        input_output_aliases={2: 0},
        compiler_params=_params(("arbitrary",), 56),
        name="out_proj",
    )(*args)


def _residual_norm(x, gate, f, g, nmod, last):
    xn = x + gate * f
    if last:
        zero = jnp.zeros_like(g)
        return xn, _rms_mod(xn, g, zero, zero)
    return xn, _rms_mod(xn, g, nmod[0:1], nmod[1:2])


def _swiglu_accumulate(x_ref, wg_ref, wu_ref, wd_ref, acc_ref):
    h = x_ref[...]
    a = _dot(h, wg_ref[...])
    act = (a * jax.nn.sigmoid(a)) * _dot(h, wu_ref[...])
    acc_ref[...] += _dot(act.astype(BF16), wd_ref[...])


def _ffn_kernel(h_ref, wg_ref, wu_ref, wd_ref, x_ref, mod_ref, g_ref, nmod_ref, xo_ref, ho_ref, *, last):
    j = pl.program_id(1)

    @pl.when(j == 0)
    def _():
        xo_ref[...] = jnp.zeros_like(xo_ref)

    _swiglu_accumulate(h_ref, wg_ref, wu_ref, wd_ref, xo_ref)

    @pl.when(j == pl.num_programs(1) - 1)
    def _():
        xn, hn = _residual_norm(x_ref[...], mod_ref[...][5:6], xo_ref[...], g_ref[...], nmod_ref[...], last)
        xo_ref[...] = xn
        ho_ref[...] = hn.astype(ho_ref.dtype)


def _ffn(h, wg, wu, wd, w_index, x, mod, layer, next_g, next_g_index, next_layer, seq,
         last=False, tm=512, tf=1024):
    n, d = x.shape
    f = wg.shape[-1]
    tiles_per_seq = seq // tm
    return pl.pallas_call(
        functools.partial(_ffn_kernel, last=last),
        grid=(n // tm, f // tf),
        in_specs=[
            pl.BlockSpec((tm, d), lambda i, j: (i, 0)),
            pl.BlockSpec((None, d, tf), lambda i, j: (w_index, 0, j)),
            pl.BlockSpec((None, d, tf), lambda i, j: (w_index, 0, j)),
            pl.BlockSpec((None, tf, d), lambda i, j: (w_index, j, 0)),
            pl.BlockSpec((tm, d), lambda i, j: (i, 0)),
            pl.BlockSpec((None, None, N_MOD, d), lambda i, j: (layer, i // tiles_per_seq, 0, 0)),
            pl.BlockSpec((None, 1, d), lambda i, j: (next_g_index, 0, 0)),
            pl.BlockSpec((None, None, N_MOD, d), lambda i, j: (next_layer, i // tiles_per_seq, 0, 0)),
        ],
        out_specs=[pl.BlockSpec((tm, d), lambda i, j: (i, 0)),
                   pl.BlockSpec((tm, d), lambda i, j: (i, 0))],
        out_shape=[jax.ShapeDtypeStruct((n, d), F32),
                   jax.ShapeDtypeStruct((n, d), F32 if last else BF16)],
        input_output_aliases={4: 0},
        compiler_params=_params(("arbitrary", "arbitrary"), 58),
        name="ffn",
    )(h, wg, wu, wd, x, mod, next_g, mod)


TOP_K = 2
ROW_DMA_UNROLL = 8


def _route_plan(route, n_experts, tm):
    n = route.shape[0]
    n_tiles = TOP_K * n // tm + n_experts - 1
    choice = route[:, ROUTE_FIRST:ROUTE_SECOND + 1].astype(jnp.int32)
    member = jnp.sum(choice[:, :, None] == jnp.arange(n_experts)[None, None, :], axis=1,
                     dtype=jnp.int32)
    rank = jnp.cumsum(member, axis=0) - member
    counts = rank[-1] + member[-1]
    padded = (counts + tm - 1) // tm * tm
    ends = jnp.cumsum(padded)
    dest = jnp.take_along_axis((ends - padded)[None, :] + rank, choice, axis=1)
    n_used = ends[-1] // tm
    tile = jnp.minimum(jnp.arange(n_tiles), n_used - 1)
    tile_expert = jnp.sum(tile[:, None] * tm >= ends[None, :], axis=1, dtype=jnp.int32)
    return dest.reshape(-1).astype(jnp.int32), tile_expert, n_used.reshape(1).astype(jnp.int32), n_tiles


def _dispatch_kernel(dest_ref, h_ref, init_ref, o_ref, sem):
    del init_ref
    tt = h_ref.shape[0]
    base = pl.program_id(0) * tt

    def row_copy(r, k):
        row = dest_ref[TOP_K * (base + r) + k]
        return pltpu.make_async_copy(h_ref.at[pl.ds(r, 1), :], o_ref.at[pl.ds(row, 1), :], sem)

    @pl.loop(0, tt, unroll=ROW_DMA_UNROLL)
    def _(r):
        for k in range(TOP_K):
            row_copy(r, k).start(priority=k)

    @pl.loop(0, tt, unroll=ROW_DMA_UNROLL)
    def _(r):
        for k in range(TOP_K):
            row_copy(r, k).wait()


def _dispatch(h, dest, n_rows, tt=256):
    n, d = h.shape
    return pl.pallas_call(
        _dispatch_kernel,
        grid_spec=pltpu.PrefetchScalarGridSpec(
            num_scalar_prefetch=1,
            grid=(n // tt,),
            in_specs=[pl.BlockSpec((tt, d), lambda i, dest: (i, 0)),
                      pl.BlockSpec(memory_space=pl.ANY)],
            out_specs=pl.BlockSpec(memory_space=pl.ANY),
            scratch_shapes=[pltpu.SemaphoreType.DMA(())],
        ),
        out_shape=jax.ShapeDtypeStruct((n_rows, d), h.dtype),
        input_output_aliases={2: 0},
        compiler_params=_params(("arbitrary",), 32),
        name="dispatch",
    )(dest, h, jnp.zeros((n_rows, d), h.dtype))


def _expert_kernel(te_ref, nu_ref, x_ref, wg_ref, wu_ref, wd_ref, y_ref, xb):
    del te_ref
    j = pl.program_id(1)
    used = pl.program_id(0) < nu_ref[0]

    @pl.when(jnp.logical_and(jnp.logical_not(used), j == 0))
    def _():
        y_ref[...] = jnp.zeros_like(y_ref)

    @pl.when(used)
    def _():
        @pl.when(j == 0)
        def _():
            xb[...] = x_ref[...].astype(BF16)
            y_ref[...] = jnp.zeros_like(y_ref)

        _swiglu_accumulate(xb, wg_ref, wu_ref, wd_ref, y_ref)


def _expert_ffn(xs, wg, wu, wd, w_base, tile_expert, n_used, n_tiles, tm, tf=1024):
    d = xs.shape[-1]
    f = wg.shape[-1]
    n_j = f // tf

    def row(i, j, te, nu):
        return (jnp.minimum(i, nu[0] - 1), 0)

    def col(i, j, nu):
        return jnp.where(i < nu[0], j, n_j - 1)

    return pl.pallas_call(
        _expert_kernel,
        grid_spec=pltpu.PrefetchScalarGridSpec(
            num_scalar_prefetch=2,
            grid=(n_tiles, n_j),
            in_specs=[
                pl.BlockSpec((tm, d), row),
                pl.BlockSpec((None, d, tf), lambda i, j, te, nu: (w_base + te[i], 0, col(i, j, nu))),
                pl.BlockSpec((None, d, tf), lambda i, j, te, nu: (w_base + te[i], 0, col(i, j, nu))),
                pl.BlockSpec((None, tf, d), lambda i, j, te, nu: (w_base + te[i], col(i, j, nu), 0)),
            ],
            out_specs=pl.BlockSpec((tm, d), lambda i, j, te, nu: (i, 0)),
            scratch_shapes=[pltpu.VMEM((tm, d), BF16)],
        ),
        out_shape=jax.ShapeDtypeStruct((n_tiles * tm, d), F32),
        compiler_params=_params(("arbitrary", "arbitrary"), 56),
        name="expert_ffn",
    )(tile_expert, n_used, xs, wg, wu, wd)


def _combine_kernel(dest_ref, y_ref, x_ref, route_ref, mod_ref, g_ref, nmod_ref, *rest, last):
    if last:
        ho_ref, ybuf, sem = rest
    else:
        xo_ref, ho_ref, ybuf, sem = rest
    tc = x_ref.shape[0]
    base = pl.program_id(0) * tc

    def row_copy(r, k):
        row = dest_ref[TOP_K * (base + r) + k]
        return pltpu.make_async_copy(y_ref.at[pl.ds(row, 1), :], ybuf.at[k, pl.ds(r, 1), :], sem.at[k])

    @pl.loop(0, tc, unroll=ROW_DMA_UNROLL)
    def _(r):
        for k in range(TOP_K):
            row_copy(r, k).start(priority=k)

    @pl.loop(0, tc, unroll=ROW_DMA_UNROLL)
    def _(r):
        for k in range(TOP_K):
            row_copy(r, k).wait()

    route = route_ref[...]
    f = (route[:, ROUTE_W_FIRST:ROUTE_W_FIRST + 1] * ybuf[0]
         + route[:, ROUTE_W_SECOND:ROUTE_W_SECOND + 1] * ybuf[1])
    xn, hn = _residual_norm(x_ref[...], mod_ref[...][5:6], f, g_ref[...], nmod_ref[...], last)
    if not last:
        xo_ref[...] = xn
    ho_ref[...] = hn.astype(ho_ref.dtype)


def _combine(y, dest, x, route, mod, layer, next_g, next_g_index, next_layer, seq, last, tc=256):
    n, d = x.shape
    tiles_per_seq = seq // tc
    tok = pl.BlockSpec((tc, d), lambda i, dest: (i, 0))
    out_specs = [tok] if last else [tok, tok]
    out_shape = ([jax.ShapeDtypeStruct((n, d), F32)] if last else
                 [jax.ShapeDtypeStruct((n, d), F32), jax.ShapeDtypeStruct((n, d), BF16)])
    return pl.pallas_call(
        functools.partial(_combine_kernel, last=last),
        grid_spec=pltpu.PrefetchScalarGridSpec(
            num_scalar_prefetch=1,
            grid=(n // tc,),
            in_specs=[
                pl.BlockSpec(memory_space=pl.ANY),
                tok,
                pl.BlockSpec((tc, V7X_LANES), lambda i, dest: (i, 0)),
                pl.BlockSpec((None, None, N_MOD, d), lambda i, dest: (layer, i // tiles_per_seq, 0, 0)),
                pl.BlockSpec((None, 1, d), lambda i, dest: (next_g_index, 0, 0)),
                pl.BlockSpec((None, None, N_MOD, d), lambda i, dest: (next_layer, i // tiles_per_seq, 0, 0)),
            ],
            out_specs=out_specs,
            scratch_shapes=[pltpu.VMEM((TOP_K, tc, d), F32), pltpu.SemaphoreType.DMA((TOP_K,))],
        ),
        out_shape=out_shape,
        input_output_aliases={} if last else {2: 0},
        compiler_params=_params(("arbitrary",), 48),
        name="combine",
    )(dest, y, x, route, mod, next_g, mod)


def _moe(h2, route, wg, wu, wd, w_base, n_experts, x, mod, layer, next_g, next_g_index, next_layer,
         seq, last, tm=512):
    dest, tile_expert, n_used, n_tiles = _route_plan(route, n_experts, tm)
    xs = _dispatch(h2, dest, n_tiles * tm)
    y = _expert_ffn(xs, wg, wu, wd, w_base, tile_expert, n_used, n_tiles, tm)
    return _combine(y, dest, x, route, mod, layer, next_g, next_g_index, next_layer, seq, last)


def _block_diag_groups(w_heads):
    depth, heads, hd, _ = w_heads.shape
    per = V7X_MXU_DIM // hd
    groups = heads // per
    wg = w_heads.reshape(depth, groups, per, hd, hd)
    eye = jnp.eye(per, dtype=w_heads.dtype)
    out = jnp.einsum("lgpde,pq->lgpdqe", wg, eye)
    return out.reshape(depth, groups, V7X_MXU_DIM, V7X_MXU_DIM)


def kernel(x, c, w_mod, b_mod, norm1_g, norm2_g, w_in, conv_a_w, conv_a_b, w_rg_a, b_rg_a, w_rg_x, b_rg_x, rg_lambda, w_out_a, conv_b_w, conv_b_b, w_out_b, w_o, w_ff_gate, w_ff_up, w_ff_down, w_router, b_router, w_e_gate, w_e_up, w_e_down, final_g):
    bsz, seq, d = x.shape
    depth = w_in.shape[0]
    n = bsz * seq
    n_moe, n_experts = w_router.shape[0], w_router.shape[-1]
    w_rnn = conv_a_w.shape[-1]

    w_o_b = w_o.astype(BF16)
    mixer_p = dict(
        conv_a_w=conv_a_w, conv_a_b=conv_a_b.reshape(depth, 1, w_rnn),
        w_rg_a=_block_diag_groups(w_rg_a).astype(BF16), b_rg_a=b_rg_a.reshape(depth, 1, w_rnn),
        w_rg_x=_block_diag_groups(w_rg_x).astype(BF16), b_rg_x=b_rg_x.reshape(depth, 1, w_rnn),
        rg_lambda=rg_lambda.reshape(depth, 1, w_rnn), w_out_a=w_out_a.astype(BF16),
        conv_b_w=conv_b_w, conv_b_b=conv_b_b.reshape(depth, 1, w_rnn), w_out_b=w_out_b.astype(BF16),
    )
    f_d, f_e = w_ff_gate.shape[-1], w_e_gate.shape[-1]
    dense_w = (w_ff_gate, w_ff_up, w_ff_down)
    expert_w = (w_e_gate.reshape(n_moe, n_experts * d, f_e), w_e_up.reshape(n_moe, n_experts * d, f_e),
                w_e_down.reshape(n_moe, n_experts * f_e, d))
    w_r = jnp.pad(w_router, ((0, 0), (0, 0), (0, V7X_LANES - n_experts))).astype(BF16)
    b_r = jnp.pad(b_router, ((0, 0), (0, V7X_LANES - n_experts)),
                  constant_values=ROUTER_PAD_BIAS).reshape(n_moe, 1, V7X_LANES)
    norm1 = norm1_g.reshape(depth, 1, d)
    norm2 = norm2_g.reshape(depth, 1, d)
    final = final_g.reshape(1, 1, d)

    mod = _modulation(c, w_mod, b_mod).reshape(depth, bsz, N_MOD, d)
    xs = x.reshape(n, d)
    h = _first_norm(xs, mod, norm1, seq)
    for l in range(depth):
        is_last = l == depth - 1
        nxt = (final, 0, l) if is_last else (norm1, l + 1, l + 1)
        j = l // 2
        if l % 2 == 0:
            z, wg, wu, wd = _in_proj(h, w_in, l, dense_w, j)
            merged = _mixer(z, mixer_p, l, bsz, seq)
            xs, h2 = _out_proj(merged, w_o_b, xs, mod, norm2, l, seq)
            xs, h = _ffn(h2, wg.reshape(1, d, f_d), wu.reshape(1, d, f_d), wd.reshape(1, f_d, d), 0,
                         xs, mod, l, *nxt, seq, last=is_last)
        else:
            z, wg, wu, wd = _in_proj(h, w_in, l, expert_w, j)
            merged = _mixer(z, mixer_p, l, bsz, seq)
            xs, h2, route = _out_proj(merged, w_o_b, xs, mod, norm2, l, seq, router=(w_r, b_r, j))
            outs = _moe(h2, route, wg.reshape(n_experts, d, f_e), wu.reshape(n_experts, d, f_e),
                        wd.reshape(n_experts, f_e, d), 0, n_experts, xs, mod, l, *nxt, seq,
                        is_last)
            if is_last:
                (h,) = outs
            else:
                xs, h = outs
    return h.reshape(bsz, seq, d)
```

```python
import functools

import jax
import jax.numpy as jnp
from jax import lax
from jax.experimental import pallas as pl
from jax.experimental.pallas import tpu as pltpu

F32 = jnp.float32
BF16 = jnp.bfloat16

V7X_LANES = 128
V7X_SUBLANES = 8
V7X_MXU_DIM = 256
V7X_VMEM_BYTES = 64 * 1024 * 1024

RG_C = 8.0
RG_CONV = 4
SC_WIDTH = 3
N_MOD = 6
EPS = 1e-6
ROUTER_PAD_BIAS = -1e30


def _params(semantics, vmem_mib):
    assert vmem_mib * 1024 * 1024 <= V7X_VMEM_BYTES
    return pltpu.CompilerParams(dimension_semantics=semantics,
                                vmem_limit_bytes=vmem_mib * 1024 * 1024)


def _dot(a, b):
    return jnp.dot(a, b, preferred_element_type=F32)


def _rms_mod(x, g, shift, scale):
    ms = jnp.mean(x * x, axis=-1, keepdims=True)
    y = x * lax.rsqrt(ms + EPS) * g
    return y * (1.0 + scale) + shift


def _mod_kernel(c_ref, w_ref, b_ref, o_ref):
    c = c_ref[...]
    c_act = (c * jax.nn.sigmoid(c)).astype(BF16)
    o_ref[...] = _dot(c_act, w_ref[...].astype(BF16)) + b_ref[...]


def _modulation(c, w_mod, b_mod, tn=1024):
    depth, d, n = w_mod.shape
    bsz = c.shape[0]
    return pl.pallas_call(
        _mod_kernel,
        grid=(depth, n // tn),
        in_specs=[
            pl.BlockSpec((bsz, d), lambda l, j: (0, 0)),
            pl.BlockSpec((None, d, tn), lambda l, j: (l, 0, j)),
            pl.BlockSpec((None, 1, tn), lambda l, j: (l, 0, j)),
        ],
        out_specs=pl.BlockSpec((None, bsz, tn), lambda l, j: (l, 0, j)),
        out_shape=jax.ShapeDtypeStruct((depth, bsz, n), F32),
        compiler_params=_params(("arbitrary", "arbitrary"), 40),
        name="modulation",
    )(c, w_mod, b_mod.reshape(depth, 1, n))


def _norm_kernel(x_ref, mod_ref, g_ref, o_ref):
    m = mod_ref[...]
    o_ref[...] = _rms_mod(x_ref[...], g_ref[...], m[0:1], m[1:2]).astype(o_ref.dtype)


def _first_norm(x, mod, norm_g, seq, tm=512):
    n, d = x.shape
    tiles_per_seq = seq // tm
    return pl.pallas_call(
        _norm_kernel,
        grid=(n // tm,),
        in_specs=[
            pl.BlockSpec((tm, d), lambda i: (i, 0)),
            pl.BlockSpec((None, None, N_MOD, d), lambda i: (0, i // tiles_per_seq, 0, 0)),
            pl.BlockSpec((None, 1, d), lambda i: (0, 0, 0)),
        ],
        out_specs=pl.BlockSpec((tm, d), lambda i: (i, 0)),
        out_shape=jax.ShapeDtypeStruct((n, d), BF16),
        compiler_params=_params(("arbitrary",), 32),
        name="first_norm",
    )(x, mod, norm_g)


ROUND_CHUNKS = 128


def _in_proj_kernel(h_ref, w_ref, *rest, n_side):
    side_in, o_ref, side_out, wb = rest[:n_side], rest[n_side], rest[n_side + 1:-1], rest[-1]

    @pl.when(pl.program_id(1) == 0)
    def _():
        wb[...] = w_ref[...].astype(BF16)

    o_ref[...] = _dot(h_ref[...], wb[...]).astype(o_ref.dtype)

    for src, dst in zip(side_in, side_out):
        dst[...] = src[...].astype(BF16)


def _in_proj(h, w_in, layer, ffn_weights, ffn_index, tm=1024, tn=1024):
    n, d = h.shape
    d_in = w_in.shape[-1]
    grid = (d_in // tn, n // tm)
    assert grid[0] * grid[1] >= ROUND_CHUNKS

    def chunk(j, i):
        return jnp.minimum(j * grid[1] + i, ROUND_CHUNKS - 1)

    side_in, side_out, side_shape = [], [], []
    for w in ffn_weights:
        _, rows, cols = w.shape
        side_in.append(pl.BlockSpec((None, rows // ROUND_CHUNKS, cols),
                                    lambda j, i: (ffn_index, chunk(j, i), 0)))
        side_out.append(pl.BlockSpec((rows // ROUND_CHUNKS, cols), lambda j, i: (chunk(j, i), 0)))
        side_shape.append(jax.ShapeDtypeStruct((rows, cols), BF16))
    return pl.pallas_call(
        functools.partial(_in_proj_kernel, n_side=len(ffn_weights)),
        grid=grid,
        in_specs=[
            pl.BlockSpec((tm, d), lambda j, i: (i, 0)),
            pl.BlockSpec((None, d, tn), lambda j, i: (layer, 0, j)),
        ] + side_in,
        out_specs=[pl.BlockSpec((tm, tn), lambda j, i: (i, j))] + side_out,
        out_shape=[jax.ShapeDtypeStruct((n, d_in), BF16)] + side_shape,
        scratch_shapes=[pltpu.VMEM((d, tn), BF16)],
        compiler_params=_params(("arbitrary", "arbitrary"), 56),
        name="in_proj",
    )(h, w_in, *ffn_weights)


def _causal_conv(x, tail_ref, w_ref, b_ref):
    ts, w = x.shape
    taps = w_ref.shape[0]
    row = lax.broadcasted_iota(jnp.int32, (V7X_SUBLANES, w), 0)
    prev = tail_ref[...]
    y = b_ref[...] + x * w_ref[taps - 1:taps, :]
    for d in range(1, taps):
        shifted = pltpu.roll(x, d, 0)
        head = jnp.where(row < d, pltpu.roll(prev, d, 0), shifted[0:V7X_SUBLANES, :])
        shifted = jnp.concatenate([head, shifted[V7X_SUBLANES:, :]], axis=0)
        y = y + shifted * w_ref[taps - 1 - d:taps - d, :]
    tail_ref[...] = x[ts - V7X_SUBLANES:ts, :]
    return y


def _mixer_kernel(ax_ref, ag_ref, bv_ref, bbg_ref, bcg_ref, ga0_ref, ga1_ref, gb0_ref, gb1_ref,
                  caw_ref, cab_ref, wra_ref, bra_ref, wrx_ref, brx_ref, lam_ref, woa_ref,
                  cbw_ref, cbb_ref, wob_ref, o_ref,
                  xbuf, cbuf, a_s, h_s, hcar):
    ts, w = ax_ref.shape
    groups = w // V7X_MXU_DIM

    @pl.when(pl.program_id(1) == 0)
    def _():
        xbuf[...] = jnp.zeros_like(xbuf)
        cbuf[...] = jnp.zeros_like(cbuf)
        hcar[...] = jnp.zeros_like(hcar)

    xc = _causal_conv(ax_ref[...].astype(F32), xbuf, caw_ref, cab_ref)

    xcb = xc.astype(BF16)

    def gate(w_ref, b_ref):
        parts = [_dot(xcb[:, g * V7X_MXU_DIM:(g + 1) * V7X_MXU_DIM], w_ref[g])
                 for g in range(groups)]
        return jax.nn.sigmoid(jnp.concatenate(parts, axis=-1) + b_ref[...])

    r = gate(wra_ref, bra_ref)
    i = gate(wrx_ref, brx_ref)
    nl = -lam_ref[...]
    softplus = jnp.maximum(nl, 0.0) + jnp.log1p(jnp.exp(-jnp.abs(nl)))
    neg_log_a = RG_C * r * softplus
    a = jnp.exp(-neg_log_a)
    v = jnp.tanh(neg_log_a) * (a * a + 1.0)
    u = jnp.where(v > 0.0, v * lax.rsqrt(v), 0.0) * (i * xc)
    a_s[...] = a
    h_s[...] = u

    row = lax.broadcasted_iota(jnp.int32, (V7X_SUBLANES, w), 0)

    def scan_group(k, carry):
        off = pl.multiple_of(k * V7X_SUBLANES, V7X_SUBLANES)
        ca = a_s[pl.ds(off, V7X_SUBLANES), :]
        cb = h_s[pl.ds(off, V7X_SUBLANES), :]
        d = 1
        while d < V7X_SUBLANES:
            keep = row >= d
            a_prev = jnp.where(keep, pltpu.roll(ca, d, 0), 1.0)
            b_prev = jnp.where(keep, pltpu.roll(cb, d, 0), 0.0)
            cb = ca * b_prev + cb
            ca = ca * a_prev
            d *= 2
        hh = ca * carry + cb
        h_s[pl.ds(off, V7X_SUBLANES), :] = hh
        return jnp.broadcast_to(hh[V7X_SUBLANES - 1:V7X_SUBLANES, :], (V7X_SUBLANES, w))

    hcar[...] = lax.fori_loop(0, ts // V7X_SUBLANES, scan_group, hcar[...], unroll=2)

    pa = (jax.nn.gelu(ag_ref[...].astype(F32), approximate=True) * h_s[...]).astype(BF16)
    ya = _dot(pa, woa_ref[...])

    cv = _causal_conv(bcg_ref[...].astype(F32) * bv_ref[...].astype(F32), cbuf, cbw_ref, cbb_ref)
    pb = (bbg_ref[...].astype(F32) * cv).astype(BF16)
    yb = _dot(pb, wob_ref[...])

    for half, (ga_ref, gb_ref) in enumerate(((ga0_ref, gb0_ref), (ga1_ref, gb1_ref))):
        cols = slice(half * w, (half + 1) * w)
        o_ref[:, cols] = (jax.nn.sigmoid(ga_ref[...].astype(F32)) * ya[:, cols]
                          + jax.nn.sigmoid(gb_ref[...].astype(F32)) * yb[:, cols]).astype(o_ref.dtype)


def _mixer(z, p, layer, bsz, seq, ts=512):
    n = z.shape[0]
    w = p["conv_a_w"].shape[-1]
    d = p["w_out_a"].shape[-1]
    assert p["conv_b_w"].shape[-1] == w and d == 2 * w and w % V7X_MXU_DIM == 0
    tiles = seq // ts

    def zcol(c):
        return pl.BlockSpec((ts, w), lambda b, s: (b * tiles + s, c))

    def vec():
        return pl.BlockSpec((None, 1, w), lambda b, s: (layer, 0, 0))

    def full(shape):
        nd = len(shape)
        return pl.BlockSpec((None,) + shape, lambda b, s: (layer,) + (0,) * nd)

    groups = w // V7X_MXU_DIM
    return pl.pallas_call(
        _mixer_kernel,
        grid=(bsz, tiles),
        in_specs=[zcol(c) for c in range(9)] + [
            full((RG_CONV, w)), vec(),
            full((groups, V7X_MXU_DIM, V7X_MXU_DIM)), vec(),
            full((groups, V7X_MXU_DIM, V7X_MXU_DIM)), vec(),
            vec(), full((w, d)),
            full((SC_WIDTH, w)), vec(), full((w, d)),
        ],
        out_specs=pl.BlockSpec((ts, d), lambda b, s: (b * tiles + s, 0)),
        out_shape=jax.ShapeDtypeStruct((n, d), BF16),
        scratch_shapes=[
            pltpu.VMEM((V7X_SUBLANES, w), F32),
            pltpu.VMEM((V7X_SUBLANES, w), F32),
            pltpu.VMEM((ts, w), F32),
            pltpu.VMEM((ts, w), F32),
            pltpu.VMEM((V7X_SUBLANES, w), F32),
        ],
        compiler_params=_params(("arbitrary", "arbitrary"), 56),
        name="mixer",
    )(*([z] * 9), p["conv_a_w"], p["conv_a_b"], p["w_rg_a"], p["b_rg_a"], p["w_rg_x"], p["b_rg_x"],
      p["rg_lambda"], p["w_out_a"], p["conv_b_w"], p["conv_b_b"], p["w_out_b"])


ROUTE_FIRST, ROUTE_SECOND, ROUTE_W_FIRST, ROUTE_W_SECOND = 0, 1, 2, 3


def _top2_route(logits):
    lane = lax.broadcasted_iota(jnp.int32, logits.shape, 1).astype(F32)
    n_lanes = float(logits.shape[-1])
    v1 = jnp.max(logits, axis=-1, keepdims=True)
    i1 = jnp.min(jnp.where(logits == v1, lane, n_lanes), axis=-1, keepdims=True)
    rest = jnp.where(lane == i1, -jnp.inf, logits)
    v2 = jnp.max(rest, axis=-1, keepdims=True)
    i2 = jnp.min(jnp.where(rest == v2, lane, n_lanes), axis=-1, keepdims=True)
    e = jnp.exp(v2 - v1)
    w1 = 1.0 / (1.0 + e)
    w2 = e / (1.0 + e)
    out = jnp.where(lane == ROUTE_FIRST, i1, 0.0)
    out = jnp.where(lane == ROUTE_SECOND, i2, out)
    out = jnp.where(lane == ROUTE_W_FIRST, w1, out)
    return jnp.where(lane == ROUTE_W_SECOND, w2, out)


OUT_PROJ_ROW_CHUNKS = 4


def _out_proj_kernel(*refs, with_router):
    if with_router:
        m_ref, wo_ref, x_ref, mod_ref, g_ref, wr_ref, br_ref, xo_ref, ho_ref, route_ref = refs
    else:
        m_ref, wo_ref, x_ref, mod_ref, g_ref, xo_ref, ho_ref = refs
    mod = mod_ref[...]
    chunk = m_ref.shape[0] // OUT_PROJ_ROW_CHUNKS
    for c in range(OUT_PROJ_ROW_CHUNKS):
        rows = pl.ds(c * chunk, chunk)
        xn = x_ref[rows, :] + mod[2:3] * _dot(m_ref[rows, :], wo_ref[...])
        xo_ref[rows, :] = xn
        h2 = _rms_mod(xn, g_ref[...], mod[3:4], mod[4:5])
        ho_ref[rows, :] = h2.astype(ho_ref.dtype)
        if with_router:
            route_ref[rows, :] = _top2_route(_dot(h2.astype(BF16), wr_ref[...]) + br_ref[...])


def _out_proj(merged, w_o, x, mod, norm_g, layer, seq, router=None, tm=512):
    n, d = x.shape
    tiles_per_seq = seq // tm
    in_specs = [
        pl.BlockSpec((tm, d), lambda i: (i, 0)),
        pl.BlockSpec((None, d, d), lambda i: (layer, 0, 0)),
        pl.BlockSpec((tm, d), lambda i: (i, 0)),
        pl.BlockSpec((None, None, N_MOD, d), lambda i: (layer, i // tiles_per_seq, 0, 0)),
        pl.BlockSpec((None, 1, d), lambda i: (layer, 0, 0)),
    ]
    out_specs = [pl.BlockSpec((tm, d), lambda i: (i, 0)), pl.BlockSpec((tm, d), lambda i: (i, 0))]
    out_shape = [jax.ShapeDtypeStruct((n, d), F32),
                 jax.ShapeDtypeStruct((n, d), BF16 if router is None else F32)]
    args = [merged, w_o, x, mod, norm_g]
    if router is not None:
        w_r, b_r, j = router
        in_specs += [pl.BlockSpec((None, d, V7X_LANES), lambda i: (j, 0, 0)),
                     pl.BlockSpec((None, 1, V7X_LANES), lambda i: (j, 0, 0))]
        out_specs.append(pl.BlockSpec((tm, V7X_LANES), lambda i: (i, 0)))
        out_shape.append(jax.ShapeDtypeStruct((n, V7X_LANES), F32))
        args += [w_r, b_r]
    return pl.pallas_call(
        functools.partial(_out_proj_kernel, with_router=router is not None),
        grid=(n // tm,),
        in_specs=in_specs, out_specs=out_specs, out_shape=out_shape,
        input_output_aliases={2: 0} if layer > 0 else {},
        compiler_params=_params(("arbitrary",), 56),
        name="out_proj",
    )(*args)


def _residual_norm(x, gate, f, g, nmod, last):
    xn = x + gate * f
    if last:
        zero = jnp.zeros_like(g)
        return xn, _rms_mod(xn, g, zero, zero)
    return xn, _rms_mod(xn, g, nmod[0:1], nmod[1:2])


def _swiglu_accumulate(x_ref, wg_ref, wu_ref, wd_ref, acc_ref):
    h = x_ref[...]
    a = _dot(h, wg_ref[...])
    act = (a * jax.nn.sigmoid(a)) * _dot(h, wu_ref[...])
    acc_ref[...] += _dot(act.astype(BF16), wd_ref[...])


def _ffn_kernel(h_ref, wg_ref, wu_ref, wd_ref, x_ref, mod_ref, g_ref, nmod_ref, xo_ref, ho_ref, *, last):
    j = pl.program_id(1)

    @pl.when(j == 0)
    def _():
        xo_ref[...] = jnp.zeros_like(xo_ref)

    _swiglu_accumulate(h_ref, wg_ref, wu_ref, wd_ref, xo_ref)

    @pl.when(j == pl.num_programs(1) - 1)
    def _():
        xn, hn = _residual_norm(x_ref[...], mod_ref[...][5:6], xo_ref[...], g_ref[...], nmod_ref[...], last)
        xo_ref[...] = xn
        ho_ref[...] = hn.astype(ho_ref.dtype)


def _ffn(h, wg, wu, wd, w_index, x, mod, layer, next_g, next_g_index, next_layer, seq,
         last=False, tm=512, tf=1024):
    n, d = x.shape
    f = wg.shape[-1]
    tiles_per_seq = seq // tm
    return pl.pallas_call(
        functools.partial(_ffn_kernel, last=last),
        grid=(n // tm, f // tf),
        in_specs=[
            pl.BlockSpec((tm, d), lambda i, j: (i, 0)),
            pl.BlockSpec((None, d, tf), lambda i, j: (w_index, 0, j)),
            pl.BlockSpec((None, d, tf), lambda i, j: (w_index, 0, j)),
            pl.BlockSpec((None, tf, d), lambda i, j: (w_index, j, 0)),
            pl.BlockSpec((tm, d), lambda i, j: (i, 0)),
            pl.BlockSpec((None, None, N_MOD, d), lambda i, j: (layer, i // tiles_per_seq, 0, 0)),
            pl.BlockSpec((None, 1, d), lambda i, j: (next_g_index, 0, 0)),
            pl.BlockSpec((None, None, N_MOD, d), lambda i, j: (next_layer, i // tiles_per_seq, 0, 0)),
        ],
        out_specs=[pl.BlockSpec((tm, d), lambda i, j: (i, 0)),
                   pl.BlockSpec((tm, d), lambda i, j: (i, 0))],
        out_shape=[jax.ShapeDtypeStruct((n, d), F32),
                   jax.ShapeDtypeStruct((n, d), F32 if last else BF16)],
        input_output_aliases={4: 0},
        compiler_params=_params(("arbitrary", "arbitrary"), 58),
        name="ffn",
    )(h, wg, wu, wd, x, mod, next_g, mod)


TOP_K = 2
ROW_DMA_UNROLL = 8


def _route_plan(route, n_experts, tm):
    n = route.shape[0]
    n_tiles = TOP_K * n // tm + n_experts - 1
    choice = route[:, ROUTE_FIRST:ROUTE_SECOND + 1].astype(jnp.int32)
    member = jnp.sum(choice[:, :, None] == jnp.arange(n_experts)[None, None, :], axis=1,
                     dtype=jnp.int32)
    rank = jnp.cumsum(member, axis=0) - member
    counts = rank[-1] + member[-1]
    padded = (counts + tm - 1) // tm * tm
    ends = jnp.cumsum(padded)
    dest = jnp.take_along_axis((ends - padded)[None, :] + rank, choice, axis=1)
    n_used = ends[-1] // tm
    tile = jnp.minimum(jnp.arange(n_tiles), n_used - 1)
    tile_expert = jnp.sum(tile[:, None] * tm >= ends[None, :], axis=1, dtype=jnp.int32)
    n_rows = n_tiles * tm
    gap_len = jnp.concatenate([padded - counts, n_rows - ends[-1:]])
    gap_start = jnp.concatenate([ends - (padded - counts), ends[-1:]])
    gap_cum = jnp.cumsum(gap_len)
    k = jnp.arange(n_rows - TOP_K * n)
    gap = jnp.sum(k[:, None] >= gap_cum[None, :], axis=1)
    free_rows = jnp.take(gap_start - (gap_cum - gap_len), gap) + k
    return (dest.reshape(-1).astype(jnp.int32), free_rows.astype(jnp.int32), tile_expert,
            n_used.reshape(1).astype(jnp.int32), n_tiles)


def _dispatch_kernel(dest_ref, free_ref, h_ref, o_ref, zrow, sem):
    tt = h_ref.shape[0]
    step = pl.program_id(0)
    base = step * tt
    free_per_step = free_ref.shape[0] // pl.num_programs(0)

    @pl.when(step == 0)
    def _():
        zrow[...] = jnp.zeros_like(zrow)

    def row_copy(r, k):
        row = dest_ref[TOP_K * (base + r) + k]
        return pltpu.make_async_copy(h_ref.at[pl.ds(r, 1), :], o_ref.at[pl.ds(row, 1), :], sem)

    def zero_copy(q):
        row = free_ref[step * free_per_step + q]
        return pltpu.make_async_copy(zrow.at[pl.ds(0, 1), :], o_ref.at[pl.ds(row, 1), :], sem)

    @pl.loop(0, tt, unroll=ROW_DMA_UNROLL)
    def _(r):
        for k in range(TOP_K):
            row_copy(r, k).start(priority=k)

    @pl.loop(0, free_per_step, unroll=ROW_DMA_UNROLL)
    def _(q):
        zero_copy(q).start()

    @pl.loop(0, tt, unroll=ROW_DMA_UNROLL)
    def _(r):
        for k in range(TOP_K):
            row_copy(r, k).wait()

    @pl.loop(0, free_per_step, unroll=ROW_DMA_UNROLL)
    def _(q):
        zero_copy(q).wait()


def _dispatch(h, dest, free_rows, n_rows, tt=256):
    n, d = h.shape
    steps = n // tt
    assert free_rows.shape[0] == n_rows - TOP_K * n and free_rows.shape[0] % (steps * ROW_DMA_UNROLL) == 0
    return pl.pallas_call(
        _dispatch_kernel,
        grid_spec=pltpu.PrefetchScalarGridSpec(
            num_scalar_prefetch=2,
            grid=(steps,),
            in_specs=[pl.BlockSpec((tt, d), lambda i, dest, free: (i, 0))],
            out_specs=pl.BlockSpec(memory_space=pl.ANY),
            scratch_shapes=[pltpu.VMEM((V7X_SUBLANES, d), h.dtype), pltpu.SemaphoreType.DMA(())],
        ),
        out_shape=jax.ShapeDtypeStruct((n_rows, d), h.dtype),
        compiler_params=_params(("arbitrary",), 32),
        name="dispatch",
    )(dest, free_rows, h)


def _expert_kernel(te_ref, nu_ref, x_ref, wg_ref, wu_ref, wd_ref, y_ref, xb):
    del te_ref
    j = pl.program_id(1)
    used = pl.program_id(0) < nu_ref[0]

    @pl.when(jnp.logical_and(jnp.logical_not(used), j == 0))
    def _():
        y_ref[...] = jnp.zeros_like(y_ref)

    @pl.when(used)
    def _():
        @pl.when(j == 0)
        def _():
            xb[...] = x_ref[...].astype(BF16)
            y_ref[...] = jnp.zeros_like(y_ref)

        _swiglu_accumulate(xb, wg_ref, wu_ref, wd_ref, y_ref)


def _expert_ffn(xs, wg, wu, wd, w_base, tile_expert, n_used, n_tiles, tm, tf=1024):
    d = xs.shape[-1]
    f = wg.shape[-1]
    n_j = f // tf

    def row(i, j, te, nu):
        return (jnp.minimum(i, nu[0] - 1), 0)

    def col(i, j, nu):
        return jnp.where(i < nu[0], j, n_j - 1)

    return pl.pallas_call(
        _expert_kernel,
        grid_spec=pltpu.PrefetchScalarGridSpec(
            num_scalar_prefetch=2,
            grid=(n_tiles, n_j),
            in_specs=[
                pl.BlockSpec((tm, d), row),
                pl.BlockSpec((None, d, tf), lambda i, j, te, nu: (w_base + te[i], 0, col(i, j, nu))),
                pl.BlockSpec((None, d, tf), lambda i, j, te, nu: (w_base + te[i], 0, col(i, j, nu))),
                pl.BlockSpec((None, tf, d), lambda i, j, te, nu: (w_base + te[i], col(i, j, nu), 0)),
            ],
            out_specs=pl.BlockSpec((tm, d), lambda i, j, te, nu: (i, 0)),
            scratch_shapes=[pltpu.VMEM((tm, d), BF16)],
        ),
        out_shape=jax.ShapeDtypeStruct((n_tiles * tm, d), F32),
        compiler_params=_params(("arbitrary", "arbitrary"), 56),
        name="expert_ffn",
    )(tile_expert, n_used, xs, wg, wu, wd)


def _combine_kernel(dest_ref, y_ref, x_ref, route_ref, mod_ref, g_ref, nmod_ref, *rest, last):
    if last:
        ho_ref, ybuf, sem = rest
    else:
        xo_ref, ho_ref, ybuf, sem = rest
    tc = x_ref.shape[0]
    base = pl.program_id(0) * tc

    def row_copy(r, k):
        row = dest_ref[TOP_K * (base + r) + k]
        return pltpu.make_async_copy(y_ref.at[pl.ds(row, 1), :], ybuf.at[k, pl.ds(r, 1), :], sem.at[k])

    @pl.loop(0, tc, unroll=ROW_DMA_UNROLL)
    def _(r):
        for k in range(TOP_K):
            row_copy(r, k).start(priority=k)

    @pl.loop(0, tc, unroll=ROW_DMA_UNROLL)
    def _(r):
        for k in range(TOP_K):
            row_copy(r, k).wait()

    route = route_ref[...]
    f = (route[:, ROUTE_W_FIRST:ROUTE_W_FIRST + 1] * ybuf[0]
         + route[:, ROUTE_W_SECOND:ROUTE_W_SECOND + 1] * ybuf[1])
    xn, hn = _residual_norm(x_ref[...], mod_ref[...][5:6], f, g_ref[...], nmod_ref[...], last)
    if not last:
        xo_ref[...] = xn
    ho_ref[...] = hn.astype(ho_ref.dtype)


def _combine(y, dest, x, route, mod, layer, next_g, next_g_index, next_layer, seq, last, tc=256):
    n, d = x.shape
    tiles_per_seq = seq // tc
    tok = pl.BlockSpec((tc, d), lambda i, dest: (i, 0))
    out_specs = [tok] if last else [tok, tok]
    out_shape = ([jax.ShapeDtypeStruct((n, d), F32)] if last else
                 [jax.ShapeDtypeStruct((n, d), F32), jax.ShapeDtypeStruct((n, d), BF16)])
    return pl.pallas_call(
        functools.partial(_combine_kernel, last=last),
        grid_spec=pltpu.PrefetchScalarGridSpec(
            num_scalar_prefetch=1,
            grid=(n // tc,),
            in_specs=[
                pl.BlockSpec(memory_space=pl.ANY),
                tok,
                pl.BlockSpec((tc, V7X_LANES), lambda i, dest: (i, 0)),
                pl.BlockSpec((None, None, N_MOD, d), lambda i, dest: (layer, i // tiles_per_seq, 0, 0)),
                pl.BlockSpec((None, 1, d), lambda i, dest: (next_g_index, 0, 0)),
                pl.BlockSpec((None, None, N_MOD, d), lambda i, dest: (next_layer, i // tiles_per_seq, 0, 0)),
            ],
            out_specs=out_specs,
            scratch_shapes=[pltpu.VMEM((TOP_K, tc, d), F32), pltpu.SemaphoreType.DMA((TOP_K,))],
        ),
        out_shape=out_shape,
        input_output_aliases={} if last else {2: 0},
        compiler_params=_params(("arbitrary",), 48),
        name="combine",
    )(dest, y, x, route, mod, next_g, mod)


def _moe(h2, route, wg, wu, wd, w_base, n_experts, x, mod, layer, next_g, next_g_index, next_layer,
         seq, last, tm=512):
    dest, free_rows, tile_expert, n_used, n_tiles = _route_plan(route, n_experts, tm)
    xs = _dispatch(h2, dest, free_rows, n_tiles * tm)
    y = _expert_ffn(xs, wg, wu, wd, w_base, tile_expert, n_used, n_tiles, tm)
    return _combine(y, dest, x, route, mod, layer, next_g, next_g_index, next_layer, seq, last)


def _block_diag_groups(w_heads):
    depth, heads, hd, _ = w_heads.shape
    per = V7X_MXU_DIM // hd
    groups = heads // per
    wg = w_heads.reshape(depth, groups, per, hd, hd)
    eye = jnp.eye(per, dtype=w_heads.dtype)
    out = jnp.einsum("lgpde,pq->lgpdqe", wg, eye)
    return out.reshape(depth, groups, V7X_MXU_DIM, V7X_MXU_DIM)


def kernel(x, c, w_mod, b_mod, norm1_g, norm2_g, w_in, conv_a_w, conv_a_b, w_rg_a, b_rg_a, w_rg_x, b_rg_x, rg_lambda, w_out_a, conv_b_w, conv_b_b, w_out_b, w_o, w_ff_gate, w_ff_up, w_ff_down, w_router, b_router, w_e_gate, w_e_up, w_e_down, final_g):
    bsz, seq, d = x.shape
    depth = w_in.shape[0]
    n = bsz * seq
    n_moe, n_experts = w_router.shape[0], w_router.shape[-1]
    w_rnn = conv_a_w.shape[-1]

    w_o_b = w_o.astype(BF16)
    mixer_p = dict(
        conv_a_w=conv_a_w, conv_a_b=conv_a_b.reshape(depth, 1, w_rnn),
        w_rg_a=_block_diag_groups(w_rg_a).astype(BF16), b_rg_a=b_rg_a.reshape(depth, 1, w_rnn),
        w_rg_x=_block_diag_groups(w_rg_x).astype(BF16), b_rg_x=b_rg_x.reshape(depth, 1, w_rnn),
        rg_lambda=rg_lambda.reshape(depth, 1, w_rnn), w_out_a=w_out_a.astype(BF16),
        conv_b_w=conv_b_w, conv_b_b=conv_b_b.reshape(depth, 1, w_rnn), w_out_b=w_out_b.astype(BF16),
    )
    f_d, f_e = w_ff_gate.shape[-1], w_e_gate.shape[-1]
    dense_w = (w_ff_gate, w_ff_up, w_ff_down)
    expert_w = (w_e_gate.reshape(n_moe, n_experts * d, f_e), w_e_up.reshape(n_moe, n_experts * d, f_e),
                w_e_down.reshape(n_moe, n_experts * f_e, d))
    w_r = jnp.pad(w_router, ((0, 0), (0, 0), (0, V7X_LANES - n_experts))).astype(BF16)
    b_r = jnp.pad(b_router, ((0, 0), (0, V7X_LANES - n_experts)),
                  constant_values=ROUTER_PAD_BIAS).reshape(n_moe, 1, V7X_LANES)
    norm1 = norm1_g.reshape(depth, 1, d)
    norm2 = norm2_g.reshape(depth, 1, d)
    final = final_g.reshape(1, 1, d)

    mod = _modulation(c, w_mod, b_mod).reshape(depth, bsz, N_MOD, d)
    xs = x.reshape(n, d)
    h = _first_norm(xs, mod, norm1, seq)
    for l in range(depth):
        is_last = l == depth - 1
        nxt = (final, 0, l) if is_last else (norm1, l + 1, l + 1)
        j = l // 2
        if l % 2 == 0:
            z, wg, wu, wd = _in_proj(h, w_in, l, dense_w, j)
            merged = _mixer(z, mixer_p, l, bsz, seq)
            xs, h2 = _out_proj(merged, w_o_b, xs, mod, norm2, l, seq)
            xs, h = _ffn(h2, wg.reshape(1, d, f_d), wu.reshape(1, d, f_d), wd.reshape(1, f_d, d), 0,
                         xs, mod, l, *nxt, seq, last=is_last)
        else:
            z, wg, wu, wd = _in_proj(h, w_in, l, expert_w, j)
            merged = _mixer(z, mixer_p, l, bsz, seq)
            xs, h2, route = _out_proj(merged, w_o_b, xs, mod, norm2, l, seq, router=(w_r, b_r, j))
            outs = _moe(h2, route, wg.reshape(n_experts, d, f_e), wu.reshape(n_experts, d, f_e),
                        wd.reshape(n_experts, f_e, d), 0, n_experts, xs, mod, l, *nxt, seq,
                        is_last)
            if is_last:
                (h,) = outs
            else:
                xs, h = outs
    return h.reshape(bsz, seq, d)
```

```python
import functools

import jax
import jax.numpy as jnp
from jax import lax
from jax.experimental import pallas as pl
from jax.experimental.pallas import tpu as pltpu

F32 = jnp.float32
BF16 = jnp.bfloat16

V7X_LANES = 128
V7X_SUBLANES = 8
V7X_MXU_DIM = 256
V7X_VMEM_BYTES = 64 * 1024 * 1024

RG_C = 8.0
RG_CONV = 4
SC_WIDTH = 3
N_MOD = 6
EPS = 1e-6
ROUTER_PAD_BIAS = -1e30


def _params(semantics, vmem_mib):
    assert vmem_mib * 1024 * 1024 <= V7X_VMEM_BYTES
    return pltpu.CompilerParams(dimension_semantics=semantics,
                                vmem_limit_bytes=vmem_mib * 1024 * 1024)


def _dot(a, b):
    return jnp.dot(a, b, preferred_element_type=F32)


def _rms_mod(x, g, shift, scale):
    ms = jnp.mean(x * x, axis=-1, keepdims=True)
    y = x * lax.rsqrt(ms + EPS) * g
    return y * (1.0 + scale) + shift


def _mod_kernel(c_ref, w_ref, b_ref, o_ref):
    c = c_ref[...]
    c_act = (c * jax.nn.sigmoid(c)).astype(BF16)
    o_ref[...] = _dot(c_act, w_ref[...].astype(BF16)) + b_ref[...]


def _modulation(c, w_mod, b_mod, tn=1024):
    depth, d, n = w_mod.shape
    bsz = c.shape[0]
    return pl.pallas_call(
        _mod_kernel,
        grid=(depth, n // tn),
        in_specs=[
            pl.BlockSpec((bsz, d), lambda l, j: (0, 0)),
            pl.BlockSpec((None, d, tn), lambda l, j: (l, 0, j)),
            pl.BlockSpec((None, 1, tn), lambda l, j: (l, 0, j)),
        ],
        out_specs=pl.BlockSpec((None, bsz, tn), lambda l, j: (l, 0, j)),
        out_shape=jax.ShapeDtypeStruct((depth, bsz, n), F32),
        compiler_params=_params(("arbitrary", "arbitrary"), 40),
        name="modulation",
    )(c, w_mod, b_mod.reshape(depth, 1, n))


def _norm_kernel(x_ref, mod_ref, g_ref, o_ref):
    m = mod_ref[...]
    o_ref[...] = _rms_mod(x_ref[...], g_ref[...], m[0:1], m[1:2]).astype(o_ref.dtype)


def _first_norm(x, mod, norm_g, seq, tm=512):
    n, d = x.shape
    tiles_per_seq = seq // tm
    return pl.pallas_call(
        _norm_kernel,
        grid=(n // tm,),
        in_specs=[
            pl.BlockSpec((tm, d), lambda i: (i, 0)),
            pl.BlockSpec((None, None, N_MOD, d), lambda i: (0, i // tiles_per_seq, 0, 0)),
            pl.BlockSpec((None, 1, d), lambda i: (0, 0, 0)),
        ],
        out_specs=pl.BlockSpec((tm, d), lambda i: (i, 0)),
        out_shape=jax.ShapeDtypeStruct((n, d), BF16),
        compiler_params=_params(("arbitrary",), 32),
        name="first_norm",
    )(x, mod, norm_g)


ROUND_CHUNKS = 128


def _in_proj_kernel(h_ref, w_ref, *rest, n_side):
    side_in, o_ref, side_out, wb = rest[:n_side], rest[n_side], rest[n_side + 1:-1], rest[-1]

    @pl.when(pl.program_id(1) == 0)
    def _():
        wb[...] = w_ref[...].astype(BF16)

    o_ref[...] = _dot(h_ref[...], wb[...]).astype(o_ref.dtype)

    for src, dst in zip(side_in, side_out):
        dst[...] = src[...].astype(BF16)


def _in_proj(h, w_in, layer, ffn_weights, ffn_index, tm=1024, tn=1024):
    n, d = h.shape
    d_in = w_in.shape[-1]
    grid = (d_in // tn, n // tm)
    assert grid[0] * grid[1] >= ROUND_CHUNKS

    def chunk(j, i):
        return jnp.minimum(j * grid[1] + i, ROUND_CHUNKS - 1)

    side_in, side_out, side_shape = [], [], []
    for w in ffn_weights:
        _, rows, cols = w.shape
        side_in.append(pl.BlockSpec((None, rows // ROUND_CHUNKS, cols),
                                    lambda j, i: (ffn_index, chunk(j, i), 0)))
        side_out.append(pl.BlockSpec((rows // ROUND_CHUNKS, cols), lambda j, i: (chunk(j, i), 0)))
        side_shape.append(jax.ShapeDtypeStruct((rows, cols), BF16))
    return pl.pallas_call(
        functools.partial(_in_proj_kernel, n_side=len(ffn_weights)),
        grid=grid,
        in_specs=[
            pl.BlockSpec((tm, d), lambda j, i: (i, 0)),
            pl.BlockSpec((None, d, tn), lambda j, i: (layer, 0, j)),
        ] + side_in,
        out_specs=[pl.BlockSpec((tm, tn), lambda j, i: (i, j))] + side_out,
        out_shape=[jax.ShapeDtypeStruct((n, d_in), BF16)] + side_shape,
        scratch_shapes=[pltpu.VMEM((d, tn), BF16)],
        compiler_params=_params(("arbitrary", "arbitrary"), 56),
        name="in_proj",
    )(h, w_in, *ffn_weights)


def _causal_conv(x, tail_ref, w_ref, b_ref):
    ts, w = x.shape
    taps = w_ref.shape[0]
    row = lax.broadcasted_iota(jnp.int32, (V7X_SUBLANES, w), 0)
    prev = tail_ref[...]
    y = b_ref[...] + x * w_ref[taps - 1:taps, :]
    for d in range(1, taps):
        shifted = pltpu.roll(x, d, 0)
        head = jnp.where(row < d, pltpu.roll(prev, d, 0), shifted[0:V7X_SUBLANES, :])
        shifted = jnp.concatenate([head, shifted[V7X_SUBLANES:, :]], axis=0)
        y = y + shifted * w_ref[taps - 1 - d:taps - d, :]
    tail_ref[...] = x[ts - V7X_SUBLANES:ts, :]
    return y


def _mixer_kernel(ax_ref, ag_ref, bv_ref, bbg_ref, bcg_ref, ga0_ref, ga1_ref, gb0_ref, gb1_ref,
                  caw_ref, cab_ref, wra_ref, bra_ref, wrx_ref, brx_ref, lam_ref, woa_ref,
                  cbw_ref, cbb_ref, wob_ref, o_ref,
                  xbuf, cbuf, a_s, h_s, hcar):
    ts, w = ax_ref.shape
    groups = w // V7X_MXU_DIM

    @pl.when(pl.program_id(1) == 0)
    def _():
        xbuf[...] = jnp.zeros_like(xbuf)
        cbuf[...] = jnp.zeros_like(cbuf)
        hcar[...] = jnp.zeros_like(hcar)

    xc = _causal_conv(ax_ref[...].astype(F32), xbuf, caw_ref, cab_ref)

    xcb = xc.astype(BF16)

    def gate(w_ref, b_ref):
        parts = [_dot(xcb[:, g * V7X_MXU_DIM:(g + 1) * V7X_MXU_DIM], w_ref[g])
                 for g in range(groups)]
        return jax.nn.sigmoid(jnp.concatenate(parts, axis=-1) + b_ref[...])

    r = gate(wra_ref, bra_ref)
    i = gate(wrx_ref, brx_ref)
    nl = -lam_ref[...]
    softplus = jnp.maximum(nl, 0.0) + jnp.log1p(jnp.exp(-jnp.abs(nl)))
    neg_log_a = RG_C * r * softplus
    a = jnp.exp(-neg_log_a)
    v = jnp.tanh(neg_log_a) * (a * a + 1.0)
    u = jnp.where(v > 0.0, v * lax.rsqrt(v), 0.0) * (i * xc)
    a_s[...] = a
    h_s[...] = u

    row = lax.broadcasted_iota(jnp.int32, (V7X_SUBLANES, w), 0)

    def scan_group(k, carry):
        off = pl.multiple_of(k * V7X_SUBLANES, V7X_SUBLANES)
        ca = a_s[pl.ds(off, V7X_SUBLANES), :]
        cb = h_s[pl.ds(off, V7X_SUBLANES), :]
        d = 1
        while d < V7X_SUBLANES:
            keep = row >= d
            a_prev = jnp.where(keep, pltpu.roll(ca, d, 0), 1.0)
            b_prev = jnp.where(keep, pltpu.roll(cb, d, 0), 0.0)
            cb = ca * b_prev + cb
            ca = ca * a_prev
            d *= 2
        hh = ca * carry + cb
        h_s[pl.ds(off, V7X_SUBLANES), :] = hh
        return jnp.broadcast_to(hh[V7X_SUBLANES - 1:V7X_SUBLANES, :], (V7X_SUBLANES, w))

    hcar[...] = lax.fori_loop(0, ts // V7X_SUBLANES, scan_group, hcar[...], unroll=2)

    pa = (jax.nn.gelu(ag_ref[...].astype(F32), approximate=True) * h_s[...]).astype(BF16)
    ya = _dot(pa, woa_ref[...])

    cv = _causal_conv(bcg_ref[...].astype(F32) * bv_ref[...].astype(F32), cbuf, cbw_ref, cbb_ref)
    pb = (bbg_ref[...].astype(F32) * cv).astype(BF16)
    yb = _dot(pb, wob_ref[...])

    for half, (ga_ref, gb_ref) in enumerate(((ga0_ref, gb0_ref), (ga1_ref, gb1_ref))):
        cols = slice(half * w, (half + 1) * w)
        o_ref[:, cols] = (jax.nn.sigmoid(ga_ref[...].astype(F32)) * ya[:, cols]
                          + jax.nn.sigmoid(gb_ref[...].astype(F32)) * yb[:, cols]).astype(o_ref.dtype)


def _mixer(z, p, layer, bsz, seq, ts=512):
    n = z.shape[0]
    w = p["conv_a_w"].shape[-1]
    d = p["w_out_a"].shape[-1]
    assert p["conv_b_w"].shape[-1] == w and d == 2 * w and w % V7X_MXU_DIM == 0
    tiles = seq // ts

    def zcol(c):
        return pl.BlockSpec((ts, w), lambda b, s: (b * tiles + s, c))

    def vec():
        return pl.BlockSpec((None, 1, w), lambda b, s: (layer, 0, 0))

    def full(shape):
        nd = len(shape)
        return pl.BlockSpec((None,) + shape, lambda b, s: (layer,) + (0,) * nd)

    groups = w // V7X_MXU_DIM
    return pl.pallas_call(
        _mixer_kernel,
        grid=(bsz, tiles),
        in_specs=[zcol(c) for c in range(9)] + [
            full((RG_CONV, w)), vec(),
            full((groups, V7X_MXU_DIM, V7X_MXU_DIM)), vec(),
            full((groups, V7X_MXU_DIM, V7X_MXU_DIM)), vec(),
            vec(), full((w, d)),
            full((SC_WIDTH, w)), vec(), full((w, d)),
        ],
        out_specs=pl.BlockSpec((ts, d), lambda b, s: (b * tiles + s, 0)),
        out_shape=jax.ShapeDtypeStruct((n, d), BF16),
        scratch_shapes=[
            pltpu.VMEM((V7X_SUBLANES, w), F32),
            pltpu.VMEM((V7X_SUBLANES, w), F32),
            pltpu.VMEM((ts, w), F32),
            pltpu.VMEM((ts, w), F32),
            pltpu.VMEM((V7X_SUBLANES, w), F32),
        ],
        compiler_params=_params(("arbitrary", "arbitrary"), 56),
        name="mixer",
    )(*([z] * 9), p["conv_a_w"], p["conv_a_b"], p["w_rg_a"], p["b_rg_a"], p["w_rg_x"], p["b_rg_x"],
      p["rg_lambda"], p["w_out_a"], p["conv_b_w"], p["conv_b_b"], p["w_out_b"])


ROUTE_FIRST, ROUTE_SECOND, ROUTE_W_FIRST, ROUTE_W_SECOND = 0, 1, 2, 3


def _top2_route(logits):
    lane = lax.broadcasted_iota(jnp.int32, logits.shape, 1).astype(F32)
    n_lanes = float(logits.shape[-1])
    v1 = jnp.max(logits, axis=-1, keepdims=True)
    i1 = jnp.min(jnp.where(logits == v1, lane, n_lanes), axis=-1, keepdims=True)
    rest = jnp.where(lane == i1, -jnp.inf, logits)
    v2 = jnp.max(rest, axis=-1, keepdims=True)
    i2 = jnp.min(jnp.where(rest == v2, lane, n_lanes), axis=-1, keepdims=True)
    e = jnp.exp(v2 - v1)
    w1 = 1.0 / (1.0 + e)
    w2 = e / (1.0 + e)
    out = jnp.where(lane == ROUTE_FIRST, i1, 0.0)
    out = jnp.where(lane == ROUTE_SECOND, i2, out)
    out = jnp.where(lane == ROUTE_W_FIRST, w1, out)
    return jnp.where(lane == ROUTE_W_SECOND, w2, out)


def _out_proj_kernel(*refs, with_router):
    if with_router:
        m_ref, wo_ref, x_ref, mod_ref, g_ref, wr_ref, br_ref, xo_ref, ho_ref, route_ref = refs
    else:
        m_ref, wo_ref, x_ref, mod_ref, g_ref, xo_ref, ho_ref = refs
    mod = mod_ref[...]
    chunk = m_ref.shape[0] // 2
    for rows in (pl.ds(0, chunk), pl.ds(chunk, chunk)):
        xn = x_ref[rows, :] + mod[2:3] * _dot(m_ref[rows, :], wo_ref[...])
        xo_ref[rows, :] = xn
        h2 = _rms_mod(xn, g_ref[...], mod[3:4], mod[4:5])
        ho_ref[rows, :] = h2.astype(ho_ref.dtype)
        if with_router:
            route_ref[rows, :] = _top2_route(_dot(h2.astype(BF16), wr_ref[...]) + br_ref[...])


def _out_proj(merged, w_o, x, mod, norm_g, layer, seq, router=None, tm=512):
    n, d = x.shape
    tiles_per_seq = seq // tm
    in_specs = [
        pl.BlockSpec((tm, d), lambda i: (i, 0)),
        pl.BlockSpec((None, d, d), lambda i: (layer, 0, 0)),
        pl.BlockSpec((tm, d), lambda i: (i, 0)),
        pl.BlockSpec((None, None, N_MOD, d), lambda i: (layer, i // tiles_per_seq, 0, 0)),
        pl.BlockSpec((None, 1, d), lambda i: (layer, 0, 0)),
    ]
    out_specs = [pl.BlockSpec((tm, d), lambda i: (i, 0)), pl.BlockSpec((tm, d), lambda i: (i, 0))]
    out_shape = [jax.ShapeDtypeStruct((n, d), F32),
                 jax.ShapeDtypeStruct((n, d), BF16 if router is None else F32)]
    args = [merged, w_o, x, mod, norm_g]
    if router is not None:
        w_r, b_r, j = router
        in_specs += [pl.BlockSpec((None, d, V7X_LANES), lambda i: (j, 0, 0)),
                     pl.BlockSpec((None, 1, V7X_LANES), lambda i: (j, 0, 0))]
        out_specs.append(pl.BlockSpec((tm, V7X_LANES), lambda i: (i, 0)))
        out_shape.append(jax.ShapeDtypeStruct((n, V7X_LANES), F32))
        args += [w_r, b_r]
    return pl.pallas_call(
        functools.partial(_out_proj_kernel, with_router=router is not None),
        grid=(n // tm,),
        in_specs=in_specs, out_specs=out_specs, out_shape=out_shape,
        input_output_aliases={2: 0} if layer > 0 else {},
        compiler_params=_params(("arbitrary",), 56),
        name="out_proj",
    )(*args)


def _residual_norm(x, gate, f, g, nmod, last):
    xn = x + gate * f
    if last:
        zero = jnp.zeros_like(g)
        return xn, _rms_mod(xn, g, zero, zero)
    return xn, _rms_mod(xn, g, nmod[0:1], nmod[1:2])


def _swiglu(h, wg_ref, wu_ref, wd_ref):
    a = _dot(h, wg_ref[...])
    act = (a * jax.nn.sigmoid(a)) * _dot(h, wu_ref[...])
    return _dot(act.astype(BF16), wd_ref[...])


def _ffn_kernel(h_ref, wg_ref, wu_ref, wd_ref, x_ref, mod_ref, g_ref, nmod_ref, xo_ref, ho_ref, *, last):
    j = pl.program_id(1)
    n_j = pl.num_programs(1)

    @pl.when(j == 0)
    def _():
        xo_ref[...] = _swiglu(h_ref[...], wg_ref, wu_ref, wd_ref)

    @pl.when(jnp.logical_and(j > 0, j < n_j - 1))
    def _():
        xo_ref[...] += _swiglu(h_ref[...], wg_ref, wu_ref, wd_ref)

    @pl.when(j == n_j - 1)
    def _():
        f = xo_ref[...] + _swiglu(h_ref[...], wg_ref, wu_ref, wd_ref)
        xn, hn = _residual_norm(x_ref[...], mod_ref[...][5:6], f, g_ref[...], nmod_ref[...], last)
        xo_ref[...] = xn
        ho_ref[...] = hn.astype(ho_ref.dtype)


def _ffn(h, wg, wu, wd, w_index, x, mod, layer, next_g, next_g_index, next_layer, seq,
         last=False, tm=512, tf=1024):
    n, d = x.shape
    f = wg.shape[-1]
    tiles_per_seq = seq // tm
    return pl.pallas_call(
        functools.partial(_ffn_kernel, last=last),
        grid=(n // tm, f // tf),
        in_specs=[
            pl.BlockSpec((tm, d), lambda i, j: (i, 0)),
            pl.BlockSpec((None, d, tf), lambda i, j: (w_index, 0, j)),
            pl.BlockSpec((None, d, tf), lambda i, j: (w_index, 0, j)),
            pl.BlockSpec((None, tf, d), lambda i, j: (w_index, j, 0)),
            pl.BlockSpec((tm, d), lambda i, j: (i, 0)),
            pl.BlockSpec((None, None, N_MOD, d), lambda i, j: (layer, i // tiles_per_seq, 0, 0)),
            pl.BlockSpec((None, 1, d), lambda i, j: (next_g_index, 0, 0)),
            pl.BlockSpec((None, None, N_MOD, d), lambda i, j: (next_layer, i // tiles_per_seq, 0, 0)),
        ],
        out_specs=[pl.BlockSpec((tm, d), lambda i, j: (i, 0)),
                   pl.BlockSpec((tm, d), lambda i, j: (i, 0))],
        out_shape=[jax.ShapeDtypeStruct((n, d), F32),
                   jax.ShapeDtypeStruct((n, d), F32 if last else BF16)],
        input_output_aliases={4: 0},
        compiler_params=_params(("arbitrary", "arbitrary"), 58),
        name="ffn",
    )(h, wg, wu, wd, x, mod, next_g, mod)


TOP_K = 2
ROW_DMA_UNROLL = 8


def _route_plan(route, n_experts, tm):
    n = route.shape[0]
    n_tiles = TOP_K * n // tm + n_experts - 1
    choice = route[:, ROUTE_FIRST:ROUTE_SECOND + 1].astype(jnp.int32)
    member = jnp.sum(choice[:, :, None] == jnp.arange(n_experts)[None, None, :], axis=1,
                     dtype=jnp.int32)
    rank = jnp.cumsum(member, axis=0) - member
    counts = rank[-1] + member[-1]
    padded = (counts + tm - 1) // tm * tm
    ends = jnp.cumsum(padded)
    dest = jnp.take_along_axis((ends - padded)[None, :] + rank, choice, axis=1)
    n_used = ends[-1] // tm
    tile = jnp.minimum(jnp.arange(n_tiles), n_used - 1)
    tile_expert = jnp.sum(tile[:, None] * tm >= ends[None, :], axis=1, dtype=jnp.int32)
    n_rows = n_tiles * tm
    gap_len = jnp.concatenate([padded - counts, n_rows - ends[-1:]])
    gap_start = jnp.concatenate([ends - (padded - counts), ends[-1:]])
    gap_cum = jnp.cumsum(gap_len)
    k = jnp.arange(n_rows - TOP_K * n)
    gap = jnp.sum(k[:, None] >= gap_cum[None, :], axis=1)
    free_rows = jnp.take(gap_start - (gap_cum - gap_len), gap) + k
    return (dest.reshape(-1).astype(jnp.int32), free_rows.astype(jnp.int32), tile_expert,
            n_used.reshape(1).astype(jnp.int32), n_tiles)


def _dispatch_kernel(dest_ref, free_ref, h_ref, o_ref, zrow, sem):
    tt = h_ref.shape[0]
    step = pl.program_id(0)
    base = step * tt
    free_per_step = free_ref.shape[0] // pl.num_programs(0)

    @pl.when(step == 0)
    def _():
        zrow[...] = jnp.zeros_like(zrow)

    def row_copy(r, k):
        row = dest_ref[TOP_K * (base + r) + k]
        return pltpu.make_async_copy(h_ref.at[pl.ds(r, 1), :], o_ref.at[pl.ds(row, 1), :], sem)

    def zero_copy(q):
        row = free_ref[step * free_per_step + q]
        return pltpu.make_async_copy(zrow.at[pl.ds(0, 1), :], o_ref.at[pl.ds(row, 1), :], sem)

    @pl.loop(0, tt, unroll=ROW_DMA_UNROLL)
    def _(r):
        for k in range(TOP_K):
            row_copy(r, k).start(priority=k)

    @pl.loop(0, free_per_step, unroll=ROW_DMA_UNROLL)
    def _(q):
        zero_copy(q).start()

    @pl.loop(0, tt, unroll=ROW_DMA_UNROLL)
    def _(r):
        for k in range(TOP_K):
            row_copy(r, k).wait()

    @pl.loop(0, free_per_step, unroll=ROW_DMA_UNROLL)
    def _(q):
        zero_copy(q).wait()


def _dispatch(h, dest, free_rows, n_rows, tt=256):
    n, d = h.shape
    steps = n // tt
    assert free_rows.shape[0] == n_rows - TOP_K * n and free_rows.shape[0] % (steps * ROW_DMA_UNROLL) == 0
    return pl.pallas_call(
        _dispatch_kernel,
        grid_spec=pltpu.PrefetchScalarGridSpec(
            num_scalar_prefetch=2,
            grid=(steps,),
            in_specs=[pl.BlockSpec((tt, d), lambda i, dest, free: (i, 0))],
            out_specs=pl.BlockSpec(memory_space=pl.ANY),
            scratch_shapes=[pltpu.VMEM((V7X_SUBLANES, d), h.dtype), pltpu.SemaphoreType.DMA(())],
        ),
        out_shape=jax.ShapeDtypeStruct((n_rows, d), h.dtype),
        compiler_params=_params(("arbitrary",), 32),
        name="dispatch",
    )(dest, free_rows, h)


def _expert_kernel(te_ref, nu_ref, x_ref, wg_ref, wu_ref, wd_ref, y_ref, xb):
    del te_ref
    j = pl.program_id(1)
    used = pl.program_id(0) < nu_ref[0]

    @pl.when(jnp.logical_and(jnp.logical_not(used), j == 0))
    def _():
        y_ref[...] = jnp.zeros_like(y_ref)

    @pl.when(jnp.logical_and(used, j == 0))
    def _():
        h = x_ref[...].astype(BF16)
        xb[...] = h
        y_ref[...] = _swiglu(h, wg_ref, wu_ref, wd_ref)

    @pl.when(jnp.logical_and(used, j > 0))
    def _():
        y_ref[...] += _swiglu(xb[...], wg_ref, wu_ref, wd_ref)


def _expert_ffn(xs, wg, wu, wd, w_base, tile_expert, n_used, n_tiles, tm, tf=1024):
    d = xs.shape[-1]
    f = wg.shape[-1]
    n_j = f // tf

    def row(i, j, te, nu):
        return (jnp.minimum(i, nu[0] - 1), 0)

    def col(i, j, nu):
        return jnp.where(i < nu[0], j, n_j - 1)

    return pl.pallas_call(
        _expert_kernel,
        grid_spec=pltpu.PrefetchScalarGridSpec(
            num_scalar_prefetch=2,
            grid=(n_tiles, n_j),
            in_specs=[
                pl.BlockSpec((tm, d), row),
                pl.BlockSpec((None, d, tf), lambda i, j, te, nu: (w_base + te[i], 0, col(i, j, nu))),
                pl.BlockSpec((None, d, tf), lambda i, j, te, nu: (w_base + te[i], 0, col(i, j, nu))),
                pl.BlockSpec((None, tf, d), lambda i, j, te, nu: (w_base + te[i], col(i, j, nu), 0)),
            ],
            out_specs=pl.BlockSpec((tm, d), lambda i, j, te, nu: (i, 0)),
            scratch_shapes=[pltpu.VMEM((tm, d), BF16)],
        ),
        out_shape=jax.ShapeDtypeStruct((n_tiles * tm, d), F32),
        compiler_params=_params(("arbitrary", "arbitrary"), 56),
        name="expert_ffn",
    )(tile_expert, n_used, xs, wg, wu, wd)


def _combine_kernel(dest_ref, y_ref, x_ref, route_ref, mod_ref, g_ref, nmod_ref, *rest, last):
    if last:
        ho_ref, ybuf, sem = rest
    else:
        xo_ref, ho_ref, ybuf, sem = rest
    tc = x_ref.shape[0]
    base = pl.program_id(0) * tc

    def row_copy(r, k):
        row = dest_ref[TOP_K * (base + r) + k]
        return pltpu.make_async_copy(y_ref.at[pl.ds(row, 1), :], ybuf.at[k, pl.ds(r, 1), :], sem.at[k])

    @pl.loop(0, tc, unroll=ROW_DMA_UNROLL)
    def _(r):
        for k in range(TOP_K):
            row_copy(r, k).start(priority=k)

    @pl.loop(0, tc, unroll=ROW_DMA_UNROLL)
    def _(r):
        for k in range(TOP_K):
            row_copy(r, k).wait()

    route = route_ref[...]
    f = (route[:, ROUTE_W_FIRST:ROUTE_W_FIRST + 1] * ybuf[0]
         + route[:, ROUTE_W_SECOND:ROUTE_W_SECOND + 1] * ybuf[1])
    xn, hn = _residual_norm(x_ref[...], mod_ref[...][5:6], f, g_ref[...], nmod_ref[...], last)
    if not last:
        xo_ref[...] = xn
    ho_ref[...] = hn.astype(ho_ref.dtype)


def _combine(y, dest, x, route, mod, layer, next_g, next_g_index, next_layer, seq, last, tc=256):
    n, d = x.shape
    tiles_per_seq = seq // tc
    tok = pl.BlockSpec((tc, d), lambda i, dest: (i, 0))
    out_specs = [tok] if last else [tok, tok]
    out_shape = ([jax.ShapeDtypeStruct((n, d), F32)] if last else
                 [jax.ShapeDtypeStruct((n, d), F32), jax.ShapeDtypeStruct((n, d), BF16)])
    return pl.pallas_call(
        functools.partial(_combine_kernel, last=last),
        grid_spec=pltpu.PrefetchScalarGridSpec(
            num_scalar_prefetch=1,
            grid=(n // tc,),
            in_specs=[
                pl.BlockSpec(memory_space=pl.ANY),
                tok,
                pl.BlockSpec((tc, V7X_LANES), lambda i, dest: (i, 0)),
                pl.BlockSpec((None, None, N_MOD, d), lambda i, dest: (layer, i // tiles_per_seq, 0, 0)),
                pl.BlockSpec((None, 1, d), lambda i, dest: (next_g_index, 0, 0)),
                pl.BlockSpec((None, None, N_MOD, d), lambda i, dest: (next_layer, i // tiles_per_seq, 0, 0)),
            ],
            out_specs=out_specs,
            scratch_shapes=[pltpu.VMEM((TOP_K, tc, d), F32), pltpu.SemaphoreType.DMA((TOP_K,))],
        ),
        out_shape=out_shape,
        input_output_aliases={} if last else {2: 0},
        compiler_params=_params(("arbitrary",), 48),
        name="combine",
    )(dest, y, x, route, mod, next_g, mod)


def _moe(h2, route, wg, wu, wd, w_base, n_experts, x, mod, layer, next_g, next_g_index, next_layer,
         seq, last, tm=512):
    dest, free_rows, tile_expert, n_used, n_tiles = _route_plan(route, n_experts, tm)
    xs = _dispatch(h2, dest, free_rows, n_tiles * tm)
    y = _expert_ffn(xs, wg, wu, wd, w_base, tile_expert, n_used, n_tiles, tm)
    return _combine(y, dest, x, route, mod, layer, next_g, next_g_index, next_layer, seq, last)


def _block_diag_groups(w_heads):
    depth, heads, hd, _ = w_heads.shape
    per = V7X_MXU_DIM // hd
    groups = heads // per
    wg = w_heads.reshape(depth, groups, per, hd, hd)
    eye = jnp.eye(per, dtype=w_heads.dtype)
    out = jnp.einsum("lgpde,pq->lgpdqe", wg, eye)
    return out.reshape(depth, groups, V7X_MXU_DIM, V7X_MXU_DIM)


def kernel(x, c, w_mod, b_mod, norm1_g, norm2_g, w_in, conv_a_w, conv_a_b, w_rg_a, b_rg_a, w_rg_x, b_rg_x, rg_lambda, w_out_a, conv_b_w, conv_b_b, w_out_b, w_o, w_ff_gate, w_ff_up, w_ff_down, w_router, b_router, w_e_gate, w_e_up, w_e_down, final_g):
    bsz, seq, d = x.shape
    depth = w_in.shape[0]
    n = bsz * seq
    n_moe, n_experts = w_router.shape[0], w_router.shape[-1]
    w_rnn = conv_a_w.shape[-1]

    w_o_b = w_o.astype(BF16)
    mixer_p = dict(
        conv_a_w=conv_a_w, conv_a_b=conv_a_b.reshape(depth, 1, w_rnn),
        w_rg_a=_block_diag_groups(w_rg_a).astype(BF16), b_rg_a=b_rg_a.reshape(depth, 1, w_rnn),
        w_rg_x=_block_diag_groups(w_rg_x).astype(BF16), b_rg_x=b_rg_x.reshape(depth, 1, w_rnn),
        rg_lambda=rg_lambda.reshape(depth, 1, w_rnn), w_out_a=w_out_a.astype(BF16),
        conv_b_w=conv_b_w, conv_b_b=conv_b_b.reshape(depth, 1, w_rnn), w_out_b=w_out_b.astype(BF16),
    )
    f_d, f_e = w_ff_gate.shape[-1], w_e_gate.shape[-1]
    dense_w = (w_ff_gate, w_ff_up, w_ff_down)
    expert_w = (w_e_gate.reshape(n_moe, n_experts * d, f_e), w_e_up.reshape(n_moe, n_experts * d, f_e),
                w_e_down.reshape(n_moe, n_experts * f_e, d))
    w_r = jnp.pad(w_router, ((0, 0), (0, 0), (0, V7X_LANES - n_experts))).astype(BF16)
    b_r = jnp.pad(b_router, ((0, 0), (0, V7X_LANES - n_experts)),
                  constant_values=ROUTER_PAD_BIAS).reshape(n_moe, 1, V7X_LANES)
    norm1 = norm1_g.reshape(depth, 1, d)
    norm2 = norm2_g.reshape(depth, 1, d)
    final = final_g.reshape(1, 1, d)

    mod = _modulation(c, w_mod, b_mod).reshape(depth, bsz, N_MOD, d)
    xs = x.reshape(n, d)
    h = _first_norm(xs, mod, norm1, seq)
    for l in range(depth):
        is_last = l == depth - 1
        nxt = (final, 0, l) if is_last else (norm1, l + 1, l + 1)
        j = l // 2
        if l % 2 == 0:
            z, wg, wu, wd = _in_proj(h, w_in, l, dense_w, j)
            merged = _mixer(z, mixer_p, l, bsz, seq)
            xs, h2 = _out_proj(merged, w_o_b, xs, mod, norm2, l, seq)
            xs, h = _ffn(h2, wg.reshape(1, d, f_d), wu.reshape(1, d, f_d), wd.reshape(1, f_d, d), 0,
                         xs, mod, l, *nxt, seq, last=is_last)
        else:
            z, wg, wu, wd = _in_proj(h, w_in, l, expert_w, j)
            merged = _mixer(z, mixer_p, l, bsz, seq)
            xs, h2, route = _out_proj(merged, w_o_b, xs, mod, norm2, l, seq, router=(w_r, b_r, j))
            outs = _moe(h2, route, wg.reshape(n_experts, d, f_e), wu.reshape(n_experts, d, f_e),
                        wd.reshape(n_experts, f_e, d), 0, n_experts, xs, mod, l, *nxt, seq,
                        is_last)
            if is_last:
                (h,) = outs
            else:
                xs, h = outs
    return h.reshape(bsz, seq, d)
```

```python
import functools

import jax
import jax.numpy as jnp
from jax import lax
from jax.experimental import pallas as pl
from jax.experimental.pallas import tpu as pltpu

F32 = jnp.float32
BF16 = jnp.bfloat16

V7X_LANES = 128
V7X_SUBLANES = 8
V7X_MXU_DIM = 256
V7X_VMEM_BYTES = 64 * 1024 * 1024

RG_C = 8.0
RG_CONV = 4
SC_WIDTH = 3
N_MOD = 6
EPS = 1e-6
ROUTER_PAD_BIAS = -1e30


def _params(semantics, vmem_mib):
    assert vmem_mib * 1024 * 1024 <= V7X_VMEM_BYTES
    return pltpu.CompilerParams(dimension_semantics=semantics,
                                vmem_limit_bytes=vmem_mib * 1024 * 1024)


def _dot(a, b):
    return jnp.dot(a, b, preferred_element_type=F32)


def _rms_mod(x, g, shift, scale):
    ms = jnp.mean(x * x, axis=-1, keepdims=True)
    y = x * lax.rsqrt(ms + EPS) * g
    return y * (1.0 + scale) + shift


def _mod_kernel(c_ref, w_ref, b_ref, o_ref):
    c = c_ref[...]
    c_act = (c * jax.nn.sigmoid(c)).astype(BF16)
    o_ref[...] = _dot(c_act, w_ref[...].astype(BF16)) + b_ref[...]


def _modulation(c, w_mod, b_mod, tn=1024):
    depth, d, n = w_mod.shape
    bsz = c.shape[0]
    return pl.pallas_call(
        _mod_kernel,
        grid=(depth, n // tn),
        in_specs=[
            pl.BlockSpec((bsz, d), lambda l, j: (0, 0)),
            pl.BlockSpec((None, d, tn), lambda l, j: (l, 0, j)),
            pl.BlockSpec((None, 1, tn), lambda l, j: (l, 0, j)),
        ],
        out_specs=pl.BlockSpec((None, bsz, tn), lambda l, j: (l, 0, j)),
        out_shape=jax.ShapeDtypeStruct((depth, bsz, n), F32),
        compiler_params=_params(("arbitrary", "arbitrary"), 40),
        name="modulation",
    )(c, w_mod, b_mod.reshape(depth, 1, n))


def _norm_kernel(x_ref, mod_ref, g_ref, o_ref):
    m = mod_ref[...]
    o_ref[...] = _rms_mod(x_ref[...], g_ref[...], m[0:1], m[1:2]).astype(o_ref.dtype)


def _first_norm(x, mod, norm_g, seq, tm=512):
    n, d = x.shape
    tiles_per_seq = seq // tm
    return pl.pallas_call(
        _norm_kernel,
        grid=(n // tm,),
        in_specs=[
            pl.BlockSpec((tm, d), lambda i: (i, 0)),
            pl.BlockSpec((None, None, N_MOD, d), lambda i: (0, i // tiles_per_seq, 0, 0)),
            pl.BlockSpec((None, 1, d), lambda i: (0, 0, 0)),
        ],
        out_specs=pl.BlockSpec((tm, d), lambda i: (i, 0)),
        out_shape=jax.ShapeDtypeStruct((n, d), BF16),
        compiler_params=_params(("arbitrary",), 32),
        name="first_norm",
    )(x, mod, norm_g)


ROUND_CHUNKS = 128


def _in_proj_kernel(h_ref, w_ref, *rest, n_side):
    side_in, o_ref, side_out, wb = rest[:n_side], rest[n_side], rest[n_side + 1:-1], rest[-1]

    @pl.when(pl.program_id(1) == 0)
    def _():
        wb[...] = w_ref[...].astype(BF16)

    o_ref[...] = _dot(h_ref[...], wb[...]).astype(o_ref.dtype)

    for src, dst in zip(side_in, side_out):
        dst[...] = src[...].astype(BF16)


def _in_proj(h, w_in, layer, side, tm=1024, tn=1024):
    n, d = h.shape
    d_in = w_in.shape[-1]
    grid = (d_in // tn, n // tm)
    assert grid[0] * grid[1] >= ROUND_CHUNKS
    bf16_rows = 2 * V7X_SUBLANES

    side_in, side_out, side_shape = [], [], []
    for w, index in side:
        _, rows, cols = w.shape
        n_chunks = min(ROUND_CHUNKS, rows // bf16_rows)
        chunk_rows = rows // n_chunks
        assert chunk_rows * n_chunks == rows and chunk_rows % bf16_rows == 0

        def chunk(j, i, n_chunks=n_chunks):
            return jnp.minimum(j * grid[1] + i, n_chunks - 1)

        side_in.append(pl.BlockSpec((None, chunk_rows, cols),
                                    lambda j, i, index=index, chunk=chunk: (index, chunk(j, i), 0)))
        side_out.append(pl.BlockSpec((chunk_rows, cols), lambda j, i, chunk=chunk: (chunk(j, i), 0)))
        side_shape.append(jax.ShapeDtypeStruct((rows, cols), BF16))
    return pl.pallas_call(
        functools.partial(_in_proj_kernel, n_side=len(side)),
        grid=grid,
        in_specs=[
            pl.BlockSpec((tm, d), lambda j, i: (i, 0)),
            pl.BlockSpec((None, d, tn), lambda j, i: (layer, 0, j)),
        ] + side_in,
        out_specs=[pl.BlockSpec((tm, tn), lambda j, i: (i, j))] + side_out,
        out_shape=[jax.ShapeDtypeStruct((n, d_in), BF16)] + side_shape,
        scratch_shapes=[pltpu.VMEM((d, tn), BF16)],
        compiler_params=_params(("arbitrary", "arbitrary"), 56),
        name="in_proj",
    )(h, w_in, *[w for w, _ in side])


def _causal_conv(x, tail_ref, w_ref, b_ref):
    ts, w = x.shape
    taps = w_ref.shape[0]
    row = lax.broadcasted_iota(jnp.int32, (V7X_SUBLANES, w), 0)
    prev = tail_ref[...]
    y = b_ref[...] + x * w_ref[taps - 1:taps, :]
    for d in range(1, taps):
        shifted = pltpu.roll(x, d, 0)
        head = jnp.where(row < d, pltpu.roll(prev, d, 0), shifted[0:V7X_SUBLANES, :])
        shifted = jnp.concatenate([head, shifted[V7X_SUBLANES:, :]], axis=0)
        y = y + shifted * w_ref[taps - 1 - d:taps - d, :]
    tail_ref[...] = x[ts - V7X_SUBLANES:ts, :]
    return y


def _mixer_kernel(ax_ref, ag_ref, bv_ref, bbg_ref, bcg_ref, ga0_ref, ga1_ref, gb0_ref, gb1_ref,
                  caw_ref, cab_ref, wra_ref, bra_ref, wrx_ref, brx_ref, lam_ref, woa_ref,
                  cbw_ref, cbb_ref, wob_ref, o_ref,
                  xbuf, cbuf, a_s, h_s, hcar):
    ts, w = ax_ref.shape
    groups = w // V7X_MXU_DIM

    @pl.when(pl.program_id(1) == 0)
    def _():
        xbuf[...] = jnp.zeros_like(xbuf)
        cbuf[...] = jnp.zeros_like(cbuf)
        hcar[...] = jnp.zeros_like(hcar)

    xc = _causal_conv(ax_ref[...].astype(F32), xbuf, caw_ref, cab_ref)

    xcb = xc.astype(BF16)

    def gate(w_ref, b_ref):
        parts = [_dot(xcb[:, g * V7X_MXU_DIM:(g + 1) * V7X_MXU_DIM], w_ref[g])
                 for g in range(groups)]
        return jax.nn.sigmoid(jnp.concatenate(parts, axis=-1) + b_ref[...])

    r = gate(wra_ref, bra_ref)
    i = gate(wrx_ref, brx_ref)
    nl = -lam_ref[...]
    softplus = jnp.maximum(nl, 0.0) + jnp.log1p(jnp.exp(-jnp.abs(nl)))
    neg_log_a = RG_C * r * softplus
    a = jnp.exp(-neg_log_a)
    v = jnp.tanh(neg_log_a) * (a * a + 1.0)
    u = jnp.where(v > 0.0, v * lax.rsqrt(v), 0.0) * (i * xc)
    a_s[...] = a
    h_s[...] = u

    row = lax.broadcasted_iota(jnp.int32, (V7X_SUBLANES, w), 0)

    def scan_group(k, carry):
        off = pl.multiple_of(k * V7X_SUBLANES, V7X_SUBLANES)
        ca = a_s[pl.ds(off, V7X_SUBLANES), :]
        cb = h_s[pl.ds(off, V7X_SUBLANES), :]
        d = 1
        while d < V7X_SUBLANES:
            keep = row >= d
            a_prev = jnp.where(keep, pltpu.roll(ca, d, 0), 1.0)
            b_prev = jnp.where(keep, pltpu.roll(cb, d, 0), 0.0)
            cb = ca * b_prev + cb
            ca = ca * a_prev
            d *= 2
        hh = ca * carry + cb
        h_s[pl.ds(off, V7X_SUBLANES), :] = hh
        return jnp.broadcast_to(hh[V7X_SUBLANES - 1:V7X_SUBLANES, :], (V7X_SUBLANES, w))

    hcar[...] = lax.fori_loop(0, ts // V7X_SUBLANES, scan_group, hcar[...], unroll=2)

    pa = (jax.nn.gelu(ag_ref[...].astype(F32), approximate=True) * h_s[...]).astype(BF16)
    ya = _dot(pa, woa_ref[...])

    cv = _causal_conv(bcg_ref[...].astype(F32) * bv_ref[...].astype(F32), cbuf, cbw_ref, cbb_ref)
    pb = (bbg_ref[...].astype(F32) * cv).astype(BF16)
    yb = _dot(pb, wob_ref[...])

    for half, (ga_ref, gb_ref) in enumerate(((ga0_ref, gb0_ref), (ga1_ref, gb1_ref))):
        cols = slice(half * w, (half + 1) * w)
        o_ref[:, cols] = (jax.nn.sigmoid(ga_ref[...].astype(F32)) * ya[:, cols]
                          + jax.nn.sigmoid(gb_ref[...].astype(F32)) * yb[:, cols]).astype(o_ref.dtype)


def _mixer(z, p, w_out_a, w_out_b, layer, bsz, seq, ts=512):
    n = z.shape[0]
    w = p["conv_a_w"].shape[-1]
    d = w_out_a.shape[-1]
    assert p["conv_b_w"].shape[-1] == w and d == 2 * w and w % V7X_MXU_DIM == 0
    tiles = seq // ts

    def zcol(c):
        return pl.BlockSpec((ts, w), lambda b, s: (b * tiles + s, c))

    def vec():
        return pl.BlockSpec((None, 1, w), lambda b, s: (layer, 0, 0))

    def full(shape):
        nd = len(shape)
        return pl.BlockSpec((None,) + shape, lambda b, s: (layer,) + (0,) * nd)

    def whole(shape):
        return pl.BlockSpec(shape, lambda b, s: (0,) * len(shape))

    groups = w // V7X_MXU_DIM
    return pl.pallas_call(
        _mixer_kernel,
        grid=(bsz, tiles),
        in_specs=[zcol(c) for c in range(9)] + [
            full((RG_CONV, w)), vec(),
            full((groups, V7X_MXU_DIM, V7X_MXU_DIM)), vec(),
            full((groups, V7X_MXU_DIM, V7X_MXU_DIM)), vec(),
            vec(), whole((w, d)),
            full((SC_WIDTH, w)), vec(), whole((w, d)),
        ],
        out_specs=pl.BlockSpec((ts, d), lambda b, s: (b * tiles + s, 0)),
        out_shape=jax.ShapeDtypeStruct((n, d), BF16),
        scratch_shapes=[
            pltpu.VMEM((V7X_SUBLANES, w), F32),
            pltpu.VMEM((V7X_SUBLANES, w), F32),
            pltpu.VMEM((ts, w), F32),
            pltpu.VMEM((ts, w), F32),
            pltpu.VMEM((V7X_SUBLANES, w), F32),
        ],
        compiler_params=_params(("arbitrary", "arbitrary"), 56),
        name="mixer",
    )(*([z] * 9), p["conv_a_w"], p["conv_a_b"], p["w_rg_a"], p["b_rg_a"], p["w_rg_x"], p["b_rg_x"],
      p["rg_lambda"], w_out_a, p["conv_b_w"], p["conv_b_b"], w_out_b)


ROUTE_FIRST, ROUTE_SECOND, ROUTE_W_FIRST, ROUTE_W_SECOND = 0, 1, 2, 3


def _top2_route(logits):
    lane = lax.broadcasted_iota(jnp.int32, logits.shape, 1).astype(F32)
    n_lanes = float(logits.shape[-1])
    v1 = jnp.max(logits, axis=-1, keepdims=True)
    i1 = jnp.min(jnp.where(logits == v1, lane, n_lanes), axis=-1, keepdims=True)
    rest = jnp.where(lane == i1, -jnp.inf, logits)
    v2 = jnp.max(rest, axis=-1, keepdims=True)
    i2 = jnp.min(jnp.where(rest == v2, lane, n_lanes), axis=-1, keepdims=True)
    e = jnp.exp(v2 - v1)
    w1 = 1.0 / (1.0 + e)
    w2 = e / (1.0 + e)
    out = jnp.where(lane == ROUTE_FIRST, i1, 0.0)
    out = jnp.where(lane == ROUTE_SECOND, i2, out)
    out = jnp.where(lane == ROUTE_W_FIRST, w1, out)
    return jnp.where(lane == ROUTE_W_SECOND, w2, out)


def _out_proj_kernel(*refs, with_router):
    if with_router:
        m_ref, wo_ref, x_ref, mod_ref, g_ref, wr_ref, br_ref, xo_ref, ho_ref, route_ref = refs
    else:
        m_ref, wo_ref, x_ref, mod_ref, g_ref, xo_ref, ho_ref = refs
    mod = mod_ref[...]
    chunk = m_ref.shape[0] // 2
    for rows in (pl.ds(0, chunk), pl.ds(chunk, chunk)):
        xn = x_ref[rows, :] + mod[2:3] * _dot(m_ref[rows, :], wo_ref[...])
        xo_ref[rows, :] = xn
        h2 = _rms_mod(xn, g_ref[...], mod[3:4], mod[4:5])
        ho_ref[rows, :] = h2.astype(ho_ref.dtype)
        if with_router:
            route_ref[rows, :] = _top2_route(_dot(h2.astype(BF16), wr_ref[...]) + br_ref[...])


def _out_proj(merged, w_o, x, mod, norm_g, layer, seq, router=None, tm=512):
    n, d = x.shape
    tiles_per_seq = seq // tm
    in_specs = [
        pl.BlockSpec((tm, d), lambda i: (i, 0)),
        pl.BlockSpec((d, d), lambda i: (0, 0)),
        pl.BlockSpec((tm, d), lambda i: (i, 0)),
        pl.BlockSpec((None, None, N_MOD, d), lambda i: (layer, i // tiles_per_seq, 0, 0)),
        pl.BlockSpec((None, 1, d), lambda i: (layer, 0, 0)),
    ]
    out_specs = [pl.BlockSpec((tm, d), lambda i: (i, 0)), pl.BlockSpec((tm, d), lambda i: (i, 0))]
    out_shape = [jax.ShapeDtypeStruct((n, d), F32),
                 jax.ShapeDtypeStruct((n, d), BF16 if router is None else F32)]
    args = [merged, w_o, x, mod, norm_g]
    if router is not None:
        w_r, b_r, j = router
        in_specs += [pl.BlockSpec((None, d, V7X_LANES), lambda i: (j, 0, 0)),
                     pl.BlockSpec((None, 1, V7X_LANES), lambda i: (j, 0, 0))]
        out_specs.append(pl.BlockSpec((tm, V7X_LANES), lambda i: (i, 0)))
        out_shape.append(jax.ShapeDtypeStruct((n, V7X_LANES), F32))
        args += [w_r, b_r]
    return pl.pallas_call(
        functools.partial(_out_proj_kernel, with_router=router is not None),
        grid=(n // tm,),
        in_specs=in_specs, out_specs=out_specs, out_shape=out_shape,
        input_output_aliases={2: 0} if layer > 0 else {},
        compiler_params=_params(("arbitrary",), 56),
        name="out_proj",
    )(*args)


def _residual_norm(x, gate, f, g, nmod, last):
    xn = x + gate * f
    if last:
        zero = jnp.zeros_like(g)
        return xn, _rms_mod(xn, g, zero, zero)
    return xn, _rms_mod(xn, g, nmod[0:1], nmod[1:2])


def _swiglu(h, wg_ref, wu_ref, wd_ref):
    a = _dot(h, wg_ref[...])
    act = (a * jax.nn.sigmoid(a)) * _dot(h, wu_ref[...])
    return _dot(act.astype(BF16), wd_ref[...])


def _ffn_kernel(h_ref, wg_ref, wu_ref, wd_ref, x_ref, mod_ref, g_ref, nmod_ref, xo_ref, ho_ref, *, last):
    j = pl.program_id(1)
    n_j = pl.num_programs(1)

    @pl.when(j == 0)
    def _():
        xo_ref[...] = _swiglu(h_ref[...], wg_ref, wu_ref, wd_ref)

    @pl.when(jnp.logical_and(j > 0, j < n_j - 1))
    def _():
        xo_ref[...] += _swiglu(h_ref[...], wg_ref, wu_ref, wd_ref)

    @pl.when(j == n_j - 1)
    def _():
        f = xo_ref[...] + _swiglu(h_ref[...], wg_ref, wu_ref, wd_ref)
        xn, hn = _residual_norm(x_ref[...], mod_ref[...][5:6], f, g_ref[...], nmod_ref[...], last)
        xo_ref[...] = xn
        ho_ref[...] = hn.astype(ho_ref.dtype)


def _ffn(h, wg, wu, wd, w_index, x, mod, layer, next_g, next_g_index, next_layer, seq,
         last=False, tm=512, tf=1024):
    n, d = x.shape
    f = wg.shape[-1]
    tiles_per_seq = seq // tm
    return pl.pallas_call(
        functools.partial(_ffn_kernel, last=last),
        grid=(n // tm, f // tf),
        in_specs=[
            pl.BlockSpec((tm, d), lambda i, j: (i, 0)),
            pl.BlockSpec((None, d, tf), lambda i, j: (w_index, 0, j)),
            pl.BlockSpec((None, d, tf), lambda i, j: (w_index, 0, j)),
            pl.BlockSpec((None, tf, d), lambda i, j: (w_index, j, 0)),
            pl.BlockSpec((tm, d), lambda i, j: (i, 0)),
            pl.BlockSpec((None, None, N_MOD, d), lambda i, j: (layer, i // tiles_per_seq, 0, 0)),
            pl.BlockSpec((None, 1, d), lambda i, j: (next_g_index, 0, 0)),
            pl.BlockSpec((None, None, N_MOD, d), lambda i, j: (next_layer, i // tiles_per_seq, 0, 0)),
        ],
        out_specs=[pl.BlockSpec((tm, d), lambda i, j: (i, 0)),
                   pl.BlockSpec((tm, d), lambda i, j: (i, 0))],
        out_shape=[jax.ShapeDtypeStruct((n, d), F32),
                   jax.ShapeDtypeStruct((n, d), F32 if last else BF16)],
        input_output_aliases={4: 0},
        compiler_params=_params(("arbitrary", "arbitrary"), 58),
        name="ffn",
    )(h, wg, wu, wd, x, mod, next_g, mod)


TOP_K = 2
ROW_DMA_UNROLL = 8


def _route_plan(route, n_experts, tm):
    n = route.shape[0]
    n_tiles = TOP_K * n // tm + n_experts - 1
    choice = route[:, ROUTE_FIRST:ROUTE_SECOND + 1].astype(jnp.int32)
    member = jnp.sum(choice[:, :, None] == jnp.arange(n_experts)[None, None, :], axis=1,
                     dtype=jnp.int32)
    rank = jnp.cumsum(member, axis=0) - member
    counts = rank[-1] + member[-1]
    padded = (counts + tm - 1) // tm * tm
    ends = jnp.cumsum(padded)
    dest = jnp.take_along_axis((ends - padded)[None, :] + rank, choice, axis=1)
    n_used = ends[-1] // tm
    tile = jnp.minimum(jnp.arange(n_tiles), n_used - 1)
    tile_expert = jnp.sum(tile[:, None] * tm >= ends[None, :], axis=1, dtype=jnp.int32)
    n_rows = n_tiles * tm
    gap_len = jnp.concatenate([padded - counts, n_rows - ends[-1:]])
    gap_start = jnp.concatenate([ends - (padded - counts), ends[-1:]])
    gap_cum = jnp.cumsum(gap_len)
    k = jnp.arange(n_rows - TOP_K * n)
    gap = jnp.sum(k[:, None] >= gap_cum[None, :], axis=1)
    free_rows = jnp.take(gap_start - (gap_cum - gap_len), gap) + k
    return (dest.reshape(-1).astype(jnp.int32), free_rows.astype(jnp.int32), tile_expert,
            n_used.reshape(1).astype(jnp.int32), n_tiles)


def _dispatch_kernel(dest_ref, free_ref, h_ref, o_ref, zrow, sem):
    tt = h_ref.shape[0]
    step = pl.program_id(0)
    base = step * tt
    free_per_step = free_ref.shape[0] // pl.num_programs(0)

    @pl.when(step == 0)
    def _():
        zrow[...] = jnp.zeros_like(zrow)

    def row_copy(r, k):
        row = dest_ref[TOP_K * (base + r) + k]
        return pltpu.make_async_copy(h_ref.at[pl.ds(r, 1), :], o_ref.at[pl.ds(row, 1), :], sem)

    def zero_copy(q):
        row = free_ref[step * free_per_step + q]
        return pltpu.make_async_copy(zrow.at[pl.ds(0, 1), :], o_ref.at[pl.ds(row, 1), :], sem)

    @pl.loop(0, tt, unroll=ROW_DMA_UNROLL)
    def _(r):
        for k in range(TOP_K):
            row_copy(r, k).start(priority=k)

    @pl.loop(0, free_per_step, unroll=ROW_DMA_UNROLL)
    def _(q):
        zero_copy(q).start()

    @pl.loop(0, tt, unroll=ROW_DMA_UNROLL)
    def _(r):
        for k in range(TOP_K):
            row_copy(r, k).wait()

    @pl.loop(0, free_per_step, unroll=ROW_DMA_UNROLL)
    def _(q):
        zero_copy(q).wait()


def _dispatch(h, dest, free_rows, n_rows, tt=256):
    n, d = h.shape
    steps = n // tt
    assert free_rows.shape[0] == n_rows - TOP_K * n and free_rows.shape[0] % (steps * ROW_DMA_UNROLL) == 0
    return pl.pallas_call(
        _dispatch_kernel,
        grid_spec=pltpu.PrefetchScalarGridSpec(
            num_scalar_prefetch=2,
            grid=(steps,),
            in_specs=[pl.BlockSpec((tt, d), lambda i, dest, free: (i, 0))],
            out_specs=pl.BlockSpec(memory_space=pl.ANY),
            scratch_shapes=[pltpu.VMEM((V7X_SUBLANES, d), h.dtype), pltpu.SemaphoreType.DMA(())],
        ),
        out_shape=jax.ShapeDtypeStruct((n_rows, d), h.dtype),
        compiler_params=_params(("arbitrary",), 32),
        name="dispatch",
    )(dest, free_rows, h)


def _expert_kernel(te_ref, nu_ref, x_ref, wg_ref, wu_ref, wd_ref, y_ref, xb):
    del te_ref
    j = pl.program_id(1)
    used = pl.program_id(0) < nu_ref[0]

    @pl.when(jnp.logical_and(jnp.logical_not(used), j == 0))
    def _():
        y_ref[...] = jnp.zeros_like(y_ref)

    @pl.when(jnp.logical_and(used, j == 0))
    def _():
        h = x_ref[...].astype(BF16)
        xb[...] = h
        y_ref[...] = _swiglu(h, wg_ref, wu_ref, wd_ref)

    @pl.when(jnp.logical_and(used, j > 0))
    def _():
        y_ref[...] += _swiglu(xb[...], wg_ref, wu_ref, wd_ref)


def _expert_ffn(xs, wg, wu, wd, w_base, tile_expert, n_used, n_tiles, tm, tf=1024):
    d = xs.shape[-1]
    f = wg.shape[-1]
    n_j = f // tf

    def row(i, j, te, nu):
        return (jnp.minimum(i, nu[0] - 1), 0)

    def col(i, j, nu):
        return jnp.where(i < nu[0], j, n_j - 1)

    return pl.pallas_call(
        _expert_kernel,
        grid_spec=pltpu.PrefetchScalarGridSpec(
            num_scalar_prefetch=2,
            grid=(n_tiles, n_j),
            in_specs=[
                pl.BlockSpec((tm, d), row),
                pl.BlockSpec((None, d, tf), lambda i, j, te, nu: (w_base + te[i], 0, col(i, j, nu))),
                pl.BlockSpec((None, d, tf), lambda i, j, te, nu: (w_base + te[i], 0, col(i, j, nu))),
                pl.BlockSpec((None, tf, d), lambda i, j, te, nu: (w_base + te[i], col(i, j, nu), 0)),
            ],
            out_specs=pl.BlockSpec((tm, d), lambda i, j, te, nu: (i, 0)),
            scratch_shapes=[pltpu.VMEM((tm, d), BF16)],
        ),
        out_shape=jax.ShapeDtypeStruct((n_tiles * tm, d), F32),
        compiler_params=_params(("arbitrary", "arbitrary"), 56),
        name="expert_ffn",
    )(tile_expert, n_used, xs, wg, wu, wd)


def _combine_kernel(dest_ref, y_ref, x_ref, route_ref, mod_ref, g_ref, nmod_ref, *rest, last):
    if last:
        ho_ref, ybuf, sem = rest
    else:
        xo_ref, ho_ref, ybuf, sem = rest
    tc = x_ref.shape[0]
    step = pl.program_id(0)
    slot = step % 2

    def request_rows(s, into):
        @pl.loop(0, tc, unroll=ROW_DMA_UNROLL)
        def _(r):
            for k in range(TOP_K):
                row = dest_ref[TOP_K * (s * tc + r) + k]
                pltpu.make_async_copy(y_ref.at[pl.ds(row, 1), :], ybuf.at[into, k, pl.ds(r, 1), :],
                                      sem.at[into, k]).start(priority=k)

    @pl.when(step == 0)
    def _():
        request_rows(0, 0)

    @pl.when(step + 1 < pl.num_programs(0))
    def _():
        request_rows(step + 1, 1 - slot)

    for k in range(TOP_K):
        pltpu.make_async_copy(y_ref.at[pl.ds(0, tc), :], ybuf.at[slot, k], sem.at[slot, k]).wait()

    route = route_ref[...]
    f = (route[:, ROUTE_W_FIRST:ROUTE_W_FIRST + 1] * ybuf[slot, 0]
         + route[:, ROUTE_W_SECOND:ROUTE_W_SECOND + 1] * ybuf[slot, 1])
    xn, hn = _residual_norm(x_ref[...], mod_ref[...][5:6], f, g_ref[...], nmod_ref[...], last)
    if not last:
        xo_ref[...] = xn
    ho_ref[...] = hn.astype(ho_ref.dtype)


def _combine(y, dest, x, route, mod, layer, next_g, next_g_index, next_layer, seq, last, tc=256):
    n, d = x.shape
    tiles_per_seq = seq // tc
    tok = pl.BlockSpec((tc, d), lambda i, dest: (i, 0))
    out_specs = [tok] if last else [tok, tok]
    out_shape = ([jax.ShapeDtypeStruct((n, d), F32)] if last else
                 [jax.ShapeDtypeStruct((n, d), F32), jax.ShapeDtypeStruct((n, d), BF16)])
    return pl.pallas_call(
        functools.partial(_combine_kernel, last=last),
        grid_spec=pltpu.PrefetchScalarGridSpec(
            num_scalar_prefetch=1,
            grid=(n // tc,),
            in_specs=[
                pl.BlockSpec(memory_space=pl.ANY),
                tok,
                pl.BlockSpec((tc, V7X_LANES), lambda i, dest: (i, 0)),
                pl.BlockSpec((None, None, N_MOD, d), lambda i, dest: (layer, i // tiles_per_seq, 0, 0)),
                pl.BlockSpec((None, 1, d), lambda i, dest: (next_g_index, 0, 0)),
                pl.BlockSpec((None, None, N_MOD, d), lambda i, dest: (next_layer, i // tiles_per_seq, 0, 0)),
            ],
            out_specs=out_specs,
            scratch_shapes=[pltpu.VMEM((2, TOP_K, tc, d), F32), pltpu.SemaphoreType.DMA((2, TOP_K))],
        ),
        out_shape=out_shape,
        input_output_aliases={} if last else {2: 0},
        compiler_params=_params(("arbitrary",), 48),
        name="combine",
    )(dest, y, x, route, mod, next_g, mod)


def _moe(h2, route, wg, wu, wd, w_base, n_experts, x, mod, layer, next_g, next_g_index, next_layer,
         seq, last, tm=512):
    dest, free_rows, tile_expert, n_used, n_tiles = _route_plan(route, n_experts, tm)
    xs = _dispatch(h2, dest, free_rows, n_tiles * tm)
    y = _expert_ffn(xs, wg, wu, wd, w_base, tile_expert, n_used, n_tiles, tm)
    return _combine(y, dest, x, route, mod, layer, next_g, next_g_index, next_layer, seq, last)


def _block_diag_groups(w_heads):
    depth, heads, hd, _ = w_heads.shape
    per = V7X_MXU_DIM // hd
    groups = heads // per
    wg = w_heads.reshape(depth, groups, per, hd, hd)
    eye = jnp.eye(per, dtype=w_heads.dtype)
    out = jnp.einsum("lgpde,pq->lgpdqe", wg, eye)
    return out.reshape(depth, groups, V7X_MXU_DIM, V7X_MXU_DIM)


def kernel(x, c, w_mod, b_mod, norm1_g, norm2_g, w_in, conv_a_w, conv_a_b, w_rg_a, b_rg_a, w_rg_x, b_rg_x, rg_lambda, w_out_a, conv_b_w, conv_b_b, w_out_b, w_o, w_ff_gate, w_ff_up, w_ff_down, w_router, b_router, w_e_gate, w_e_up, w_e_down, final_g):
    bsz, seq, d = x.shape
    depth = w_in.shape[0]
    n = bsz * seq
    n_moe, n_experts = w_router.shape[0], w_router.shape[-1]
    w_rnn = conv_a_w.shape[-1]

    mixer_p = dict(
        conv_a_w=conv_a_w, conv_a_b=conv_a_b.reshape(depth, 1, w_rnn),
        w_rg_a=_block_diag_groups(w_rg_a).astype(BF16), b_rg_a=b_rg_a.reshape(depth, 1, w_rnn),
        w_rg_x=_block_diag_groups(w_rg_x).astype(BF16), b_rg_x=b_rg_x.reshape(depth, 1, w_rnn),
        rg_lambda=rg_lambda.reshape(depth, 1, w_rnn),
        conv_b_w=conv_b_w, conv_b_b=conv_b_b.reshape(depth, 1, w_rnn),
    )
    f_d, f_e = w_ff_gate.shape[-1], w_e_gate.shape[-1]
    dense_w = (w_ff_gate, w_ff_up, w_ff_down)
    expert_w = (w_e_gate.reshape(n_moe, n_experts * d, f_e), w_e_up.reshape(n_moe, n_experts * d, f_e),
                w_e_down.reshape(n_moe, n_experts * f_e, d))
    w_r = jnp.pad(w_router, ((0, 0), (0, 0), (0, V7X_LANES - n_experts))).astype(BF16)
    b_r = jnp.pad(b_router, ((0, 0), (0, V7X_LANES - n_experts)),
                  constant_values=ROUTER_PAD_BIAS).reshape(n_moe, 1, V7X_LANES)
    norm1 = norm1_g.reshape(depth, 1, d)
    norm2 = norm2_g.reshape(depth, 1, d)
    final = final_g.reshape(1, 1, d)

    mod = _modulation(c, w_mod, b_mod).reshape(depth, bsz, N_MOD, d)
    xs = x.reshape(n, d)
    h = _first_norm(xs, mod, norm1, seq)
    for l in range(depth):
        is_last = l == depth - 1
        nxt = (final, 0, l) if is_last else (norm1, l + 1, l + 1)
        j = l // 2
        layer_w = [(w_out_a, l), (w_out_b, l), (w_o, l)]
        if l % 2 == 0:
            z, woa, wob, wo, wg, wu, wd = _in_proj(h, w_in, l, layer_w + [(w, j) for w in dense_w])
            merged = _mixer(z, mixer_p, woa, wob, l, bsz, seq)
            xs, h2 = _out_proj(merged, wo, xs, mod, norm2, l, seq)
            xs, h = _ffn(h2, wg.reshape(1, d, f_d), wu.reshape(1, d, f_d), wd.reshape(1, f_d, d), 0,
                         xs, mod, l, *nxt, seq, last=is_last)
        else:
            z, woa, wob, wo, wg, wu, wd = _in_proj(h, w_in, l, layer_w + [(w, j) for w in expert_w])
            merged = _mixer(z, mixer_p, woa, wob, l, bsz, seq)
            xs, h2, route = _out_proj(merged, wo, xs, mod, norm2, l, seq, router=(w_r, b_r, j))
            outs = _moe(h2, route, wg.reshape(n_experts, d, f_e), wu.reshape(n_experts, d, f_e),
                        wd.reshape(n_experts, f_e, d), 0, n_experts, xs, mod, l, *nxt, seq,
                        is_last)
            if is_last:
                (h,) = outs
            else:
                xs, h = outs
    return h.reshape(bsz, seq, d)
```

```python
import functools

import jax
import jax.numpy as jnp
from jax import lax
from jax.experimental import pallas as pl
from jax.experimental.pallas import tpu as pltpu

F32 = jnp.float32
BF16 = jnp.bfloat16

V7X_LANES = 128
V7X_SUBLANES = 8
V7X_MXU_DIM = 256
V7X_VMEM_BYTES = 64 * 1024 * 1024

RG_C = 8.0
RG_CONV = 4
SC_WIDTH = 3
N_MOD = 6
EPS = 1e-6
ROUTER_PAD_BIAS = -1e30


def _params(semantics, vmem_mib):
    assert vmem_mib * 1024 * 1024 <= V7X_VMEM_BYTES
    return pltpu.CompilerParams(dimension_semantics=semantics,
                                vmem_limit_bytes=vmem_mib * 1024 * 1024)


def _dot(a, b):
    return jnp.dot(a, b, preferred_element_type=F32)


def _rms_mod(x, g, shift, scale):
    ms = jnp.mean(x * x, axis=-1, keepdims=True)
    y = x * lax.rsqrt(ms + EPS) * g
    return y * (1.0 + scale) + shift


def _mod_kernel(c_ref, w_ref, b_ref, o_ref):
    c = c_ref[...]
    c_act = (c * jax.nn.sigmoid(c)).astype(BF16)
    o_ref[...] = _dot(c_act, w_ref[...].astype(BF16)) + b_ref[...]


def _modulation(c, w_mod, b_mod, tn=1024):
    depth, d, n = w_mod.shape
    bsz = c.shape[0]
    return pl.pallas_call(
        _mod_kernel,
        grid=(depth, n // tn),
        in_specs=[
            pl.BlockSpec((bsz, d), lambda l, j: (0, 0)),
            pl.BlockSpec((None, d, tn), lambda l, j: (l, 0, j)),
            pl.BlockSpec((None, 1, tn), lambda l, j: (l, 0, j)),
        ],
        out_specs=pl.BlockSpec((None, bsz, tn), lambda l, j: (l, 0, j)),
        out_shape=jax.ShapeDtypeStruct((depth, bsz, n), F32),
        compiler_params=_params(("arbitrary", "arbitrary"), 40),
        name="modulation",
    )(c, w_mod, b_mod.reshape(depth, 1, n))


def _norm_kernel(x_ref, mod_ref, g_ref, o_ref):
    m = mod_ref[...]
    o_ref[...] = _rms_mod(x_ref[...], g_ref[...], m[0:1], m[1:2]).astype(o_ref.dtype)


def _first_norm(x, mod, norm_g, seq, tm=512):
    n, d = x.shape
    tiles_per_seq = seq // tm
    return pl.pallas_call(
        _norm_kernel,
        grid=(n // tm,),
        in_specs=[
            pl.BlockSpec((tm, d), lambda i: (i, 0)),
            pl.BlockSpec((None, None, N_MOD, d), lambda i: (0, i // tiles_per_seq, 0, 0)),
            pl.BlockSpec((None, 1, d), lambda i: (0, 0, 0)),
        ],
        out_specs=pl.BlockSpec((tm, d), lambda i: (i, 0)),
        out_shape=jax.ShapeDtypeStruct((n, d), BF16),
        compiler_params=_params(("arbitrary",), 32),
        name="first_norm",
    )(x, mod, norm_g)


ROUND_CHUNKS = 128


def _in_proj_kernel(h_ref, w_ref, *rest, n_side):
    side_in, o_ref, side_out, wb = rest[:n_side], rest[n_side], rest[n_side + 1:-1], rest[-1]

    @pl.when(pl.program_id(1) == 0)
    def _():
        wb[...] = w_ref[...].astype(BF16)

    o_ref[...] = _dot(h_ref[...], wb[...]).astype(o_ref.dtype)

    for src, dst in zip(side_in, side_out):
        dst[...] = src[...].astype(BF16)


def _in_proj(h, w_in, layer, side, tm=1024, tn=1024):
    n, d = h.shape
    d_in = w_in.shape[-1]
    grid = (d_in // tn, n // tm)
    assert grid[0] * grid[1] >= ROUND_CHUNKS
    bf16_rows = 2 * V7X_SUBLANES

    side_in, side_out, side_shape = [], [], []
    for w, index in side:
        _, rows, cols = w.shape
        n_chunks = min(ROUND_CHUNKS, rows // bf16_rows)
        chunk_rows = rows // n_chunks
        assert chunk_rows * n_chunks == rows and chunk_rows % bf16_rows == 0

        def chunk(j, i, n_chunks=n_chunks):
            return jnp.minimum(j * grid[1] + i, n_chunks - 1)

        side_in.append(pl.BlockSpec((None, chunk_rows, cols),
                                    lambda j, i, index=index, chunk=chunk: (index, chunk(j, i), 0)))
        side_out.append(pl.BlockSpec((chunk_rows, cols), lambda j, i, chunk=chunk: (chunk(j, i), 0)))
        side_shape.append(jax.ShapeDtypeStruct((rows, cols), BF16))
    return pl.pallas_call(
        functools.partial(_in_proj_kernel, n_side=len(side)),
        grid=grid,
        in_specs=[
            pl.BlockSpec((tm, d), lambda j, i: (i, 0)),
            pl.BlockSpec((None, d, tn), lambda j, i: (layer, 0, j)),
        ] + side_in,
        out_specs=[pl.BlockSpec((tm, tn), lambda j, i: (i, j))] + side_out,
        out_shape=[jax.ShapeDtypeStruct((n, d_in), BF16)] + side_shape,
        scratch_shapes=[pltpu.VMEM((d, tn), BF16)],
        compiler_params=_params(("arbitrary", "arbitrary"), 56),
        name="in_proj",
    )(h, w_in, *[w for w, _ in side])


def _causal_conv(x, tail_ref, w_ref, b_ref):
    ts, w = x.shape
    taps = w_ref.shape[0]
    row = lax.broadcasted_iota(jnp.int32, (V7X_SUBLANES, w), 0)
    prev = tail_ref[...]
    y = b_ref[...] + x * w_ref[taps - 1:taps, :]
    for d in range(1, taps):
        shifted = pltpu.roll(x, d, 0)
        head = jnp.where(row < d, pltpu.roll(prev, d, 0), shifted[0:V7X_SUBLANES, :])
        shifted = jnp.concatenate([head, shifted[V7X_SUBLANES:, :]], axis=0)
        y = y + shifted * w_ref[taps - 1 - d:taps - d, :]
    tail_ref[...] = x[ts - V7X_SUBLANES:ts, :]
    return y


def _mixer_kernel(ax_ref, ag_ref, bv_ref, bbg_ref, bcg_ref, ga0_ref, ga1_ref, gb0_ref, gb1_ref,
                  caw_ref, cab_ref, wra_ref, bra_ref, wrx_ref, brx_ref, lam_ref, woa_ref,
                  cbw_ref, cbb_ref, wob_ref, o_ref,
                  xbuf, cbuf, a_s, h_s, hcar):
    ts, w = ax_ref.shape
    groups = w // V7X_MXU_DIM

    @pl.when(pl.program_id(1) == 0)
    def _():
        xbuf[...] = jnp.zeros_like(xbuf)
        cbuf[...] = jnp.zeros_like(cbuf)
        hcar[...] = jnp.zeros_like(hcar)

    xc = _causal_conv(ax_ref[...].astype(F32), xbuf, caw_ref, cab_ref)

    xcb = xc.astype(BF16)

    def gate(w_ref, b_ref):
        parts = [_dot(xcb[:, g * V7X_MXU_DIM:(g + 1) * V7X_MXU_DIM], w_ref[g])
                 for g in range(groups)]
        return jax.nn.sigmoid(jnp.concatenate(parts, axis=-1) + b_ref[...])

    r = gate(wra_ref, bra_ref)
    i = gate(wrx_ref, brx_ref)
    nl = -lam_ref[...]
    softplus = jnp.maximum(nl, 0.0) + jnp.log1p(jnp.exp(-jnp.abs(nl)))
    neg_log_a = RG_C * r * softplus
    a = jnp.exp(-neg_log_a)
    v = jnp.tanh(neg_log_a) * (a * a + 1.0)
    u = jnp.where(v > 0.0, v * lax.rsqrt(v), 0.0) * (i * xc)
    n_seg = V7X_SUBLANES
    seg = ts // n_seg
    pitch = a_s.shape[1] // n_seg
    slabs = w // V7X_LANES
    for l in range(slabs):
        lanes = slice(l * V7X_LANES, (l + 1) * V7X_LANES)
        for s in range(n_seg):
            a_s[l, s * pitch:s * pitch + seg, :] = a[s * seg:(s + 1) * seg, lanes]
            h_s[l, s * pitch:s * pitch + seg, :] = u[s * seg:(s + 1) * seg, lanes]

    def sweep(k, carry):
        h_loc, a_cum = carry
        h_new, a_new = [], []
        for l in range(slabs):
            rows = pl.ds(k, n_seg, stride=pitch)
            ak = a_s[l, rows, :]
            hk = ak * h_loc[l] + h_s[l, rows, :]
            pk = a_cum[l] * ak
            h_s[l, rows, :] = hk
            a_s[l, rows, :] = pk
            h_new.append(hk)
            a_new.append(pk)
        return h_new, a_new

    seg_shape = (n_seg, V7X_LANES)
    h_end, a_end = lax.fori_loop(
        0, seg, sweep,
        ([jnp.zeros(seg_shape, F32)] * slabs, [jnp.ones(seg_shape, F32)] * slabs), unroll=8)

    row = lax.broadcasted_iota(jnp.int32, seg_shape, 0)
    enter = []
    for l in range(slabs):
        ae, he = a_end[l], h_end[l]
        d = 1
        while d < n_seg:
            keep = row >= d
            a_prev = jnp.where(keep, pltpu.roll(ae, d, 0), 1.0)
            h_prev = jnp.where(keep, pltpu.roll(he, d, 0), 0.0)
            he = ae * h_prev + he
            ae = ae * a_prev
            d *= 2
        h_in = hcar[l]
        h_out = ae * h_in + he
        enter.append(jnp.where(row == 0, h_in, pltpu.roll(h_out, 1, 0)))
        hcar[l] = jnp.broadcast_to(h_out[n_seg - 1:n_seg, :], seg_shape)

    hr = jnp.concatenate(
        [jnp.concatenate([h_s[l, s * pitch:s * pitch + seg, :]
                          + a_s[l, s * pitch:s * pitch + seg, :] * enter[l][s:s + 1, :]
                          for l in range(slabs)], axis=1)
         for s in range(n_seg)], axis=0)
    pa = (jax.nn.gelu(ag_ref[...].astype(F32), approximate=True) * hr).astype(BF16)
    ya = _dot(pa, woa_ref[...])

    cv = _causal_conv(bcg_ref[...].astype(F32) * bv_ref[...].astype(F32), cbuf, cbw_ref, cbb_ref)
    pb = (bbg_ref[...].astype(F32) * cv).astype(BF16)
    yb = _dot(pb, wob_ref[...])

    for half, (ga_ref, gb_ref) in enumerate(((ga0_ref, gb0_ref), (ga1_ref, gb1_ref))):
        cols = slice(half * w, (half + 1) * w)
        o_ref[:, cols] = (jax.nn.sigmoid(ga_ref[...].astype(F32)) * ya[:, cols]
                          + jax.nn.sigmoid(gb_ref[...].astype(F32)) * yb[:, cols]).astype(o_ref.dtype)


def _mixer(z, p, w_out_a, w_out_b, layer, bsz, seq, ts=512):
    n = z.shape[0]
    w = p["conv_a_w"].shape[-1]
    d = w_out_a.shape[-1]
    assert p["conv_b_w"].shape[-1] == w and d == 2 * w and w % V7X_MXU_DIM == 0
    tiles = seq // ts
    seg = ts // V7X_SUBLANES
    seg_pitch = seg if (seg // V7X_SUBLANES) % 2 else seg + V7X_SUBLANES

    def zcol(c):
        return pl.BlockSpec((ts, w), lambda b, s: (b * tiles + s, c))

    def vec():
        return pl.BlockSpec((None, 1, w), lambda b, s: (layer, 0, 0))

    def full(shape):
        nd = len(shape)
        return pl.BlockSpec((None,) + shape, lambda b, s: (layer,) + (0,) * nd)

    def whole(shape):
        return pl.BlockSpec(shape, lambda b, s: (0,) * len(shape))

    groups = w // V7X_MXU_DIM
    return pl.pallas_call(
        _mixer_kernel,
        grid=(bsz, tiles),
        in_specs=[zcol(c) for c in range(9)] + [
            full((RG_CONV, w)), vec(),
            full((groups, V7X_MXU_DIM, V7X_MXU_DIM)), vec(),
            full((groups, V7X_MXU_DIM, V7X_MXU_DIM)), vec(),
            vec(), whole((w, d)),
            full((SC_WIDTH, w)), vec(), whole((w, d)),
        ],
        out_specs=pl.BlockSpec((ts, d), lambda b, s: (b * tiles + s, 0)),
        out_shape=jax.ShapeDtypeStruct((n, d), BF16),
        scratch_shapes=[
            pltpu.VMEM((V7X_SUBLANES, w), F32),
            pltpu.VMEM((V7X_SUBLANES, w), F32),
            pltpu.VMEM((w // V7X_LANES, V7X_SUBLANES * seg_pitch, V7X_LANES), F32),
            pltpu.VMEM((w // V7X_LANES, V7X_SUBLANES * seg_pitch, V7X_LANES), F32),
            pltpu.VMEM((w // V7X_LANES, V7X_SUBLANES, V7X_LANES), F32),
        ],
        compiler_params=_params(("arbitrary", "arbitrary"), 56),
        name="mixer",
    )(*([z] * 9), p["conv_a_w"], p["conv_a_b"], p["w_rg_a"], p["b_rg_a"], p["w_rg_x"], p["b_rg_x"],
      p["rg_lambda"], w_out_a, p["conv_b_w"], p["conv_b_b"], w_out_b)


ROUTE_FIRST, ROUTE_SECOND, ROUTE_W_FIRST, ROUTE_W_SECOND = 0, 1, 2, 3


def _top2_route(logits):
    lane = lax.broadcasted_iota(jnp.int32, logits.shape, 1).astype(F32)
    n_lanes = float(logits.shape[-1])
    v1 = jnp.max(logits, axis=-1, keepdims=True)
    i1 = jnp.min(jnp.where(logits == v1, lane, n_lanes), axis=-1, keepdims=True)
    rest = jnp.where(lane == i1, -jnp.inf, logits)
    v2 = jnp.max(rest, axis=-1, keepdims=True)
    i2 = jnp.min(jnp.where(rest == v2, lane, n_lanes), axis=-1, keepdims=True)
    e = jnp.exp(v2 - v1)
    w1 = 1.0 / (1.0 + e)
    w2 = e / (1.0 + e)
    out = jnp.where(lane == ROUTE_FIRST, i1, 0.0)
    out = jnp.where(lane == ROUTE_SECOND, i2, out)
    out = jnp.where(lane == ROUTE_W_FIRST, w1, out)
    return jnp.where(lane == ROUTE_W_SECOND, w2, out)


def _out_proj_kernel(*refs, with_router):
    if with_router:
        m_ref, wo_ref, x_ref, mod_ref, g_ref, wr_ref, br_ref, xo_ref, ho_ref, route_ref = refs
    else:
        m_ref, wo_ref, x_ref, mod_ref, g_ref, xo_ref, ho_ref = refs
    mod = mod_ref[...]
    chunk = m_ref.shape[0] // 2
    for rows in (pl.ds(0, chunk), pl.ds(chunk, chunk)):
        xn = x_ref[rows, :] + mod[2:3] * _dot(m_ref[rows, :], wo_ref[...])
        xo_ref[rows, :] = xn
        h2 = _rms_mod(xn, g_ref[...], mod[3:4], mod[4:5])
        ho_ref[rows, :] = h2.astype(ho_ref.dtype)
        if with_router:
            route_ref[rows, :] = _top2_route(_dot(h2.astype(BF16), wr_ref[...]) + br_ref[...])


def _out_proj(merged, w_o, x, mod, norm_g, layer, seq, router=None, tm=512):
    n, d = x.shape
    tiles_per_seq = seq // tm
    in_specs = [
        pl.BlockSpec((tm, d), lambda i: (i, 0)),
        pl.BlockSpec((d, d), lambda i: (0, 0)),
        pl.BlockSpec((tm, d), lambda i: (i, 0)),
        pl.BlockSpec((None, None, N_MOD, d), lambda i: (layer, i // tiles_per_seq, 0, 0)),
        pl.BlockSpec((None, 1, d), lambda i: (layer, 0, 0)),
    ]
    out_specs = [pl.BlockSpec((tm, d), lambda i: (i, 0)), pl.BlockSpec((tm, d), lambda i: (i, 0))]
    out_shape = [jax.ShapeDtypeStruct((n, d), F32),
                 jax.ShapeDtypeStruct((n, d), BF16 if router is None else F32)]
    args = [merged, w_o, x, mod, norm_g]
    if router is not None:
        w_r, b_r, j = router
        in_specs += [pl.BlockSpec((None, d, V7X_LANES), lambda i: (j, 0, 0)),
                     pl.BlockSpec((None, 1, V7X_LANES), lambda i: (j, 0, 0))]
        out_specs.append(pl.BlockSpec((tm, V7X_LANES), lambda i: (i, 0)))
        out_shape.append(jax.ShapeDtypeStruct((n, V7X_LANES), F32))
        args += [w_r, b_r]
    return pl.pallas_call(
        functools.partial(_out_proj_kernel, with_router=router is not None),
        grid=(n // tm,),
        in_specs=in_specs, out_specs=out_specs, out_shape=out_shape,
        input_output_aliases={2: 0} if layer > 0 else {},
        compiler_params=_params(("arbitrary",), 56),
        name="out_proj",
    )(*args)


def _residual_norm(x, gate, f, g, nmod, last):
    xn = x + gate * f
    if last:
        zero = jnp.zeros_like(g)
        return xn, _rms_mod(xn, g, zero, zero)
    return xn, _rms_mod(xn, g, nmod[0:1], nmod[1:2])


def _swiglu(h, wg_ref, wu_ref, wd_ref):
    a = _dot(h, wg_ref[...])
    act = (a * jax.nn.sigmoid(a)) * _dot(h, wu_ref[...])
    return _dot(act.astype(BF16), wd_ref[...])


def _ffn_kernel(h_ref, wg_ref, wu_ref, wd_ref, x_ref, mod_ref, g_ref, nmod_ref, xo_ref, ho_ref, *, last):
    j = pl.program_id(1)
    n_j = pl.num_programs(1)

    @pl.when(j == 0)
    def _():
        xo_ref[...] = _swiglu(h_ref[...], wg_ref, wu_ref, wd_ref)

    @pl.when(jnp.logical_and(j > 0, j < n_j - 1))
    def _():
        xo_ref[...] += _swiglu(h_ref[...], wg_ref, wu_ref, wd_ref)

    @pl.when(j == n_j - 1)
    def _():
        f = xo_ref[...] + _swiglu(h_ref[...], wg_ref, wu_ref, wd_ref)
        xn, hn = _residual_norm(x_ref[...], mod_ref[...][5:6], f, g_ref[...], nmod_ref[...], last)
        xo_ref[...] = xn
        ho_ref[...] = hn.astype(ho_ref.dtype)


def _ffn(h, wg, wu, wd, w_index, x, mod, layer, next_g, next_g_index, next_layer, seq,
         last=False, tm=512, tf=1024):
    n, d = x.shape
    f = wg.shape[-1]
    tiles_per_seq = seq // tm
    return pl.pallas_call(
        functools.partial(_ffn_kernel, last=last),
        grid=(n // tm, f // tf),
        in_specs=[
            pl.BlockSpec((tm, d), lambda i, j: (i, 0)),
            pl.BlockSpec((None, d, tf), lambda i, j: (w_index, 0, j)),
            pl.BlockSpec((None, d, tf), lambda i, j: (w_index, 0, j)),
            pl.BlockSpec((None, tf, d), lambda i, j: (w_index, j, 0)),
            pl.BlockSpec((tm, d), lambda i, j: (i, 0)),
            pl.BlockSpec((None, None, N_MOD, d), lambda i, j: (layer, i // tiles_per_seq, 0, 0)),
            pl.BlockSpec((None, 1, d), lambda i, j: (next_g_index, 0, 0)),
            pl.BlockSpec((None, None, N_MOD, d), lambda i, j: (next_layer, i // tiles_per_seq, 0, 0)),
        ],
        out_specs=[pl.BlockSpec((tm, d), lambda i, j: (i, 0)),
                   pl.BlockSpec((tm, d), lambda i, j: (i, 0))],
        out_shape=[jax.ShapeDtypeStruct((n, d), F32),
                   jax.ShapeDtypeStruct((n, d), F32 if last else BF16)],
        input_output_aliases={4: 0},
        compiler_params=_params(("arbitrary", "arbitrary"), 58),
        name="ffn",
    )(h, wg, wu, wd, x, mod, next_g, mod)


TOP_K = 2
ROW_DMA_UNROLL = 8


def _route_plan(route, n_experts, tm):
    n = route.shape[0]
    n_tiles = TOP_K * n // tm + n_experts - 1
    choice = route[:, ROUTE_FIRST:ROUTE_SECOND + 1].astype(jnp.int32)
    member = jnp.sum(choice[:, :, None] == jnp.arange(n_experts)[None, None, :], axis=1,
                     dtype=jnp.int32)
    rank = jnp.cumsum(member, axis=0) - member
    counts = rank[-1] + member[-1]
    padded = (counts + tm - 1) // tm * tm
    ends = jnp.cumsum(padded)
    dest = jnp.take_along_axis((ends - padded)[None, :] + rank, choice, axis=1)
    n_used = ends[-1] // tm
    tile = jnp.minimum(jnp.arange(n_tiles), n_used - 1)
    tile_expert = jnp.sum(tile[:, None] * tm >= ends[None, :], axis=1, dtype=jnp.int32)
    n_rows = n_tiles * tm
    gap_len = jnp.concatenate([padded - counts, n_rows - ends[-1:]])
    gap_start = jnp.concatenate([ends - (padded - counts), ends[-1:]])
    gap_cum = jnp.cumsum(gap_len)
    k = jnp.arange(n_rows - TOP_K * n)
    gap = jnp.sum(k[:, None] >= gap_cum[None, :], axis=1)
    free_rows = jnp.take(gap_start - (gap_cum - gap_len), gap) + k
    return (dest.reshape(-1).astype(jnp.int32), free_rows.astype(jnp.int32), tile_expert,
            n_used.reshape(1).astype(jnp.int32), n_tiles)


def _dispatch_kernel(dest_ref, free_ref, h_ref, o_ref, zrow, sem, *, free_per_step):
    tt = h_ref.shape[0]
    step = pl.program_id(0)
    base = step * tt

    @pl.when(step == 0)
    def _():
        zrow[...] = jnp.zeros_like(zrow)

    def row_copy(r, k):
        row = dest_ref[TOP_K * (base + r) + k]
        return pltpu.make_async_copy(h_ref.at[pl.ds(r, 1), :], o_ref.at[pl.ds(row, 1), :], sem)

    def zero_copy(q):
        row = free_ref[step * free_per_step + q]
        return pltpu.make_async_copy(zrow.at[pl.ds(0, 1), :], o_ref.at[pl.ds(row, 1), :], sem)

    @pl.loop(0, tt, unroll=ROW_DMA_UNROLL)
    def _(r):
        for k in range(TOP_K):
            row_copy(r, k).start(priority=k)

    @pl.loop(0, free_per_step, unroll=ROW_DMA_UNROLL)
    def _(q):
        zero_copy(q).start()

    for _ in range(TOP_K):
        pltpu.make_async_copy(h_ref, o_ref.at[pl.ds(0, tt), :], sem).wait()
    pltpu.make_async_copy(h_ref.at[pl.ds(0, free_per_step), :],
                          o_ref.at[pl.ds(0, free_per_step), :], sem).wait()


def _dispatch(h, dest, free_rows, n_rows, tt=256):
    n, d = h.shape
    steps = n // tt
    assert free_rows.shape[0] == n_rows - TOP_K * n and free_rows.shape[0] % (steps * ROW_DMA_UNROLL) == 0
    return pl.pallas_call(
        functools.partial(_dispatch_kernel, free_per_step=free_rows.shape[0] // steps),
        grid_spec=pltpu.PrefetchScalarGridSpec(
            num_scalar_prefetch=2,
            grid=(steps,),
            in_specs=[pl.BlockSpec((tt, d), lambda i, dest, free: (i, 0))],
            out_specs=pl.BlockSpec(memory_space=pl.ANY),
            scratch_shapes=[pltpu.VMEM((V7X_SUBLANES, d), h.dtype), pltpu.SemaphoreType.DMA(())],
        ),
        out_shape=jax.ShapeDtypeStruct((n_rows, d), h.dtype),
        compiler_params=_params(("arbitrary",), 32),
        name="dispatch",
    )(dest, free_rows, h)


def _expert_kernel(te_ref, nu_ref, x_ref, wg_ref, wu_ref, wd_ref, y_ref, xb):
    del te_ref
    j = pl.program_id(1)
    used = pl.program_id(0) < nu_ref[0]

    @pl.when(jnp.logical_and(jnp.logical_not(used), j == 0))
    def _():
        y_ref[...] = jnp.zeros_like(y_ref)

    @pl.when(jnp.logical_and(used, j == 0))
    def _():
        h = x_ref[...].astype(BF16)
        xb[...] = h
        y_ref[...] = _swiglu(h, wg_ref, wu_ref, wd_ref)

    @pl.when(jnp.logical_and(used, j > 0))
    def _():
        y_ref[...] += _swiglu(xb[...], wg_ref, wu_ref, wd_ref)


def _expert_ffn(xs, wg, wu, wd, w_base, tile_expert, n_used, n_tiles, tm, tf=1024):
    d = xs.shape[-1]
    f = wg.shape[-1]
    n_j = f // tf

    def row(i, j, te, nu):
        return (jnp.minimum(i, nu[0] - 1), 0)

    def col(i, j, nu):
        return jnp.where(i < nu[0], j, n_j - 1)

    return pl.pallas_call(
        _expert_kernel,
        grid_spec=pltpu.PrefetchScalarGridSpec(
            num_scalar_prefetch=2,
            grid=(n_tiles, n_j),
            in_specs=[
                pl.BlockSpec((tm, d), row),
                pl.BlockSpec((None, d, tf), lambda i, j, te, nu: (w_base + te[i], 0, col(i, j, nu))),
                pl.BlockSpec((None, d, tf), lambda i, j, te, nu: (w_base + te[i], 0, col(i, j, nu))),
                pl.BlockSpec((None, tf, d), lambda i, j, te, nu: (w_base + te[i], col(i, j, nu), 0)),
            ],
            out_specs=pl.BlockSpec((tm, d), lambda i, j, te, nu: (i, 0)),
            scratch_shapes=[pltpu.VMEM((tm, d), BF16)],
        ),
        out_shape=jax.ShapeDtypeStruct((n_tiles * tm, d), F32),
        compiler_params=_params(("arbitrary", "arbitrary"), 56),
        name="expert_ffn",
    )(tile_expert, n_used, xs, wg, wu, wd)


def _combine_kernel(dest_ref, y_ref, x_ref, route_ref, mod_ref, g_ref, nmod_ref, *rest, last):
    if last:
        ho_ref, ybuf, sem = rest
    else:
        xo_ref, ho_ref, ybuf, sem = rest
    tc = x_ref.shape[0]
    step = pl.program_id(0)
    slot = step % 2

    def request_rows(s, into):
        @pl.loop(0, tc, unroll=ROW_DMA_UNROLL)
        def _(r):
            for k in range(TOP_K):
                row = dest_ref[TOP_K * (s * tc + r) + k]
                pltpu.make_async_copy(y_ref.at[pl.ds(row, 1), :], ybuf.at[into, k, pl.ds(r, 1), :],
                                      sem.at[into, k]).start(priority=k)

    @pl.when(step == 0)
    def _():
        request_rows(0, 0)

    @pl.when(step + 1 < pl.num_programs(0))
    def _():
        request_rows(step + 1, 1 - slot)

    for k in range(TOP_K):
        pltpu.make_async_copy(y_ref.at[pl.ds(0, tc), :], ybuf.at[slot, k], sem.at[slot, k]).wait()

    route = route_ref[...]
    f = (route[:, ROUTE_W_FIRST:ROUTE_W_FIRST + 1] * ybuf[slot, 0]
         + route[:, ROUTE_W_SECOND:ROUTE_W_SECOND + 1] * ybuf[slot, 1])
    xn, hn = _residual_norm(x_ref[...], mod_ref[...][5:6], f, g_ref[...], nmod_ref[...], last)
    if not last:
        xo_ref[...] = xn
    ho_ref[...] = hn.astype(ho_ref.dtype)


def _combine(y, dest, x, route, mod, layer, next_g, next_g_index, next_layer, seq, last, tc=256):
    n, d = x.shape
    tiles_per_seq = seq // tc
    tok = pl.BlockSpec((tc, d), lambda i, dest: (i, 0))
    out_specs = [tok] if last else [tok, tok]
    out_shape = ([jax.ShapeDtypeStruct((n, d), F32)] if last else
                 [jax.ShapeDtypeStruct((n, d), F32), jax.ShapeDtypeStruct((n, d), BF16)])
    return pl.pallas_call(
        functools.partial(_combine_kernel, last=last),
        grid_spec=pltpu.PrefetchScalarGridSpec(
            num_scalar_prefetch=1,
            grid=(n // tc,),
            in_specs=[
                pl.BlockSpec(memory_space=pl.ANY),
                tok,
                pl.BlockSpec((tc, V7X_LANES), lambda i, dest: (i, 0)),
                pl.BlockSpec((None, None, N_MOD, d), lambda i, dest: (layer, i // tiles_per_seq, 0, 0)),
                pl.BlockSpec((None, 1, d), lambda i, dest: (next_g_index, 0, 0)),
                pl.BlockSpec((None, None, N_MOD, d), lambda i, dest: (next_layer, i // tiles_per_seq, 0, 0)),
            ],
            out_specs=out_specs,
            scratch_shapes=[pltpu.VMEM((2, TOP_K, tc, d), F32), pltpu.SemaphoreType.DMA((2, TOP_K))],
        ),
        out_shape=out_shape,
        input_output_aliases={} if last else {2: 0},
        compiler_params=_params(("arbitrary",), 48),
        name="combine",
    )(dest, y, x, route, mod, next_g, mod)


def _moe(h2, route, wg, wu, wd, w_base, n_experts, x, mod, layer, next_g, next_g_index, next_layer,
         seq, last, tm=512):
    dest, free_rows, tile_expert, n_used, n_tiles = _route_plan(route, n_experts, tm)
    xs = _dispatch(h2, dest, free_rows, n_tiles * tm)
    y = _expert_ffn(xs, wg, wu, wd, w_base, tile_expert, n_used, n_tiles, tm)
    return _combine(y, dest, x, route, mod, layer, next_g, next_g_index, next_layer, seq, last)


def _block_diag_groups(w_heads):
    depth, heads, hd, _ = w_heads.shape
    per = V7X_MXU_DIM // hd
    groups = heads // per
    wg = w_heads.reshape(depth, groups, per, hd, hd)
    eye = jnp.eye(per, dtype=w_heads.dtype)
    out = jnp.einsum("lgpde,pq->lgpdqe", wg, eye)
    return out.reshape(depth, groups, V7X_MXU_DIM, V7X_MXU_DIM)


def kernel(x, c, w_mod, b_mod, norm1_g, norm2_g, w_in, conv_a_w, conv_a_b, w_rg_a, b_rg_a, w_rg_x, b_rg_x, rg_lambda, w_out_a, conv_b_w, conv_b_b, w_out_b, w_o, w_ff_gate, w_ff_up, w_ff_down, w_router, b_router, w_e_gate, w_e_up, w_e_down, final_g):
    bsz, seq, d = x.shape
    depth = w_in.shape[0]
    n = bsz * seq
    n_moe, n_experts = w_router.shape[0], w_router.shape[-1]
    w_rnn = conv_a_w.shape[-1]

    mixer_p = dict(
        conv_a_w=conv_a_w, conv_a_b=conv_a_b.reshape(depth, 1, w_rnn),
        w_rg_a=_block_diag_groups(w_rg_a).astype(BF16), b_rg_a=b_rg_a.reshape(depth, 1, w_rnn),
        w_rg_x=_block_diag_groups(w_rg_x).astype(BF16), b_rg_x=b_rg_x.reshape(depth, 1, w_rnn),
        rg_lambda=rg_lambda.reshape(depth, 1, w_rnn),
        conv_b_w=conv_b_w, conv_b_b=conv_b_b.reshape(depth, 1, w_rnn),
    )
    f_d, f_e = w_ff_gate.shape[-1], w_e_gate.shape[-1]
    dense_w = (w_ff_gate, w_ff_up, w_ff_down)
    expert_w = (w_e_gate.reshape(n_moe, n_experts * d, f_e), w_e_up.reshape(n_moe, n_experts * d, f_e),
                w_e_down.reshape(n_moe, n_experts * f_e, d))
    w_r = jnp.pad(w_router, ((0, 0), (0, 0), (0, V7X_LANES - n_experts))).astype(BF16)
    b_r = jnp.pad(b_router, ((0, 0), (0, V7X_LANES - n_experts)),
                  constant_values=ROUTER_PAD_BIAS).reshape(n_moe, 1, V7X_LANES)
    norm1 = norm1_g.reshape(depth, 1, d)
    norm2 = norm2_g.reshape(depth, 1, d)
    final = final_g.reshape(1, 1, d)

    mod = _modulation(c, w_mod, b_mod).reshape(depth, bsz, N_MOD, d)
    xs = x.reshape(n, d)
    h = _first_norm(xs, mod, norm1, seq)
    for l in range(depth):
        is_last = l == depth - 1
        nxt = (final, 0, l) if is_last else (norm1, l + 1, l + 1)
        j = l // 2
        layer_w = [(w_out_a, l), (w_out_b, l), (w_o, l)]
        if l % 2 == 0:
            z, woa, wob, wo, wg, wu, wd = _in_proj(h, w_in, l, layer_w + [(w, j) for w in dense_w])
            merged = _mixer(z, mixer_p, woa, wob, l, bsz, seq)
            xs, h2 = _out_proj(merged, wo, xs, mod, norm2, l, seq)
            xs, h = _ffn(h2, wg.reshape(1, d, f_d), wu.reshape(1, d, f_d), wd.reshape(1, f_d, d), 0,
                         xs, mod, l, *nxt, seq, last=is_last)
        else:
            z, woa, wob, wo, wg, wu, wd = _in_proj(h, w_in, l, layer_w + [(w, j) for w in expert_w])
            merged = _mixer(z, mixer_p, woa, wob, l, bsz, seq)
            xs, h2, route = _out_proj(merged, wo, xs, mod, norm2, l, seq, router=(w_r, b_r, j))
            outs = _moe(h2, route, wg.reshape(n_experts, d, f_e), wu.reshape(n_experts, d, f_e),
                        wd.reshape(n_experts, f_e, d), 0, n_experts, xs, mod, l, *nxt, seq,
                        is_last)
            if is_last:
                (h,) = outs
            else:
                xs, h = outs
    return h.reshape(bsz, seq, d)
```

```python
import functools

import jax
import jax.numpy as jnp
from jax import lax
from jax.experimental import pallas as pl
from jax.experimental.pallas import tpu as pltpu

F32 = jnp.float32
BF16 = jnp.bfloat16

V7X_LANES = 128
V7X_SUBLANES = 8
V7X_MXU_DIM = 256
V7X_VMEM_BYTES = 64 * 1024 * 1024

RG_C = 8.0
RG_CONV = 4
SC_WIDTH = 3
N_MOD = 6
EPS = 1e-6
ROUTER_PAD_BIAS = -1e30


def _params(semantics, vmem_mib):
    assert vmem_mib * 1024 * 1024 <= V7X_VMEM_BYTES
    return pltpu.CompilerParams(dimension_semantics=semantics,
                                vmem_limit_bytes=vmem_mib * 1024 * 1024)


def _dot(a, b):
    return jnp.dot(a, b, preferred_element_type=F32)


def _rms_mod(x, g, shift, scale):
    ms = jnp.mean(x * x, axis=-1, keepdims=True)
    y = x * lax.rsqrt(ms + EPS) * g
    return y * (1.0 + scale) + shift


def _mod_kernel(c_ref, w_ref, b_ref, o_ref):
    c = c_ref[...]
    c_act = (c * jax.nn.sigmoid(c)).astype(BF16)
    o_ref[...] = _dot(c_act, w_ref[...].astype(BF16)) + b_ref[...]


def _modulation(c, w_mod, b_mod, tn=1024):
    depth, d, n = w_mod.shape
    bsz = c.shape[0]
    return pl.pallas_call(
        _mod_kernel,
        grid=(depth, n // tn),
        in_specs=[
            pl.BlockSpec((bsz, d), lambda l, j: (0, 0)),
            pl.BlockSpec((None, d, tn), lambda l, j: (l, 0, j)),
            pl.BlockSpec((None, 1, tn), lambda l, j: (l, 0, j)),
        ],
        out_specs=pl.BlockSpec((None, bsz, tn), lambda l, j: (l, 0, j)),
        out_shape=jax.ShapeDtypeStruct((depth, bsz, n), F32),
        compiler_params=_params(("arbitrary", "arbitrary"), 40),
        name="modulation",
    )(c, w_mod, b_mod.reshape(depth, 1, n))


def _norm_kernel(x_ref, mod_ref, g_ref, o_ref):
    m = mod_ref[...]
    o_ref[...] = _rms_mod(x_ref[...], g_ref[...], m[0:1], m[1:2]).astype(o_ref.dtype)


def _first_norm(x, mod, norm_g, seq, tm=512):
    n, d = x.shape
    tiles_per_seq = seq // tm
    return pl.pallas_call(
        _norm_kernel,
        grid=(n // tm,),
        in_specs=[
            pl.BlockSpec((tm, d), lambda i: (i, 0)),
            pl.BlockSpec((None, None, N_MOD, d), lambda i: (0, i // tiles_per_seq, 0, 0)),
            pl.BlockSpec((None, 1, d), lambda i: (0, 0, 0)),
        ],
        out_specs=pl.BlockSpec((tm, d), lambda i: (i, 0)),
        out_shape=jax.ShapeDtypeStruct((n, d), BF16),
        compiler_params=_params(("arbitrary",), 32),
        name="first_norm",
    )(x, mod, norm_g)


ROUND_CHUNKS = 128


def _in_proj_kernel(h_ref, w_ref, *rest, n_side):
    side_in, o_ref, side_out, wb = rest[:n_side], rest[n_side], rest[n_side + 1:-1], rest[-1]

    @pl.when(pl.program_id(1) == 0)
    def _():
        wb[...] = w_ref[...].astype(BF16)

    o_ref[...] = _dot(h_ref[...], wb[...]).astype(o_ref.dtype)

    for src, dst in zip(side_in, side_out):
        dst[...] = src[...].astype(BF16)


def _in_proj(h, w_in, layer, side, tm=1024, tn=1024):
    n, d = h.shape
    d_in = w_in.shape[-1]
    grid = (d_in // tn, n // tm)
    assert grid[0] * grid[1] >= ROUND_CHUNKS
    bf16_rows = 2 * V7X_SUBLANES

    side_in, side_out, side_shape = [], [], []
    for w, index in side:
        _, rows, cols = w.shape
        n_chunks = min(ROUND_CHUNKS, rows // bf16_rows)
        chunk_rows = rows // n_chunks
        assert chunk_rows * n_chunks == rows and chunk_rows % bf16_rows == 0

        def chunk(j, i, n_chunks=n_chunks):
            return jnp.minimum(j * grid[1] + i, n_chunks - 1)

        side_in.append(pl.BlockSpec((None, chunk_rows, cols),
                                    lambda j, i, index=index, chunk=chunk: (index, chunk(j, i), 0)))
        side_out.append(pl.BlockSpec((chunk_rows, cols), lambda j, i, chunk=chunk: (chunk(j, i), 0)))
        side_shape.append(jax.ShapeDtypeStruct((rows, cols), BF16))
    return pl.pallas_call(
        functools.partial(_in_proj_kernel, n_side=len(side)),
        grid=grid,
        in_specs=[
            pl.BlockSpec((tm, d), lambda j, i: (i, 0)),
            pl.BlockSpec((None, d, tn), lambda j, i: (layer, 0, j)),
        ] + side_in,
        out_specs=[pl.BlockSpec((tm, tn), lambda j, i: (i, j))] + side_out,
        out_shape=[jax.ShapeDtypeStruct((n, d_in), BF16)] + side_shape,
        scratch_shapes=[pltpu.VMEM((d, tn), BF16)],
        compiler_params=_params(("arbitrary", "arbitrary"), 56),
        name="in_proj",
    )(h, w_in, *[w for w, _ in side])


def _causal_conv(x, tail_ref, w_ref, b_ref):
    ts, w = x.shape
    taps = w_ref.shape[0]
    row = lax.broadcasted_iota(jnp.int32, (V7X_SUBLANES, w), 0)
    prev = tail_ref[...]
    y = b_ref[...] + x * w_ref[taps - 1:taps, :]
    for d in range(1, taps):
        shifted = pltpu.roll(x, d, 0)
        head = jnp.where(row < d, pltpu.roll(prev, d, 0), shifted[0:V7X_SUBLANES, :])
        shifted = jnp.concatenate([head, shifted[V7X_SUBLANES:, :]], axis=0)
        y = y + shifted * w_ref[taps - 1 - d:taps - d, :]
    tail_ref[...] = x[ts - V7X_SUBLANES:ts, :]
    return y


def _mixer_kernel(ax_ref, ag_ref, bv_ref, bbg_ref, bcg_ref, ga0_ref, ga1_ref, gb0_ref, gb1_ref,
                  caw_ref, cab_ref, wra_ref, bra_ref, wrx_ref, brx_ref, lam_ref, woa_ref,
                  cbw_ref, cbb_ref, wob_ref, o_ref,
                  xbuf, cbuf, a_s, h_s, hcar):
    ts, w = ax_ref.shape
    groups = w // V7X_MXU_DIM

    @pl.when(pl.program_id(1) == 0)
    def _():
        xbuf[...] = jnp.zeros_like(xbuf)
        cbuf[...] = jnp.zeros_like(cbuf)
        hcar[...] = jnp.zeros_like(hcar)

    xc = _causal_conv(ax_ref[...].astype(F32), xbuf, caw_ref, cab_ref)

    xcb = xc.astype(BF16)

    def gate(w_ref, b_ref):
        parts = [_dot(xcb[:, g * V7X_MXU_DIM:(g + 1) * V7X_MXU_DIM], w_ref[g])
                 for g in range(groups)]
        return jax.nn.sigmoid(jnp.concatenate(parts, axis=-1) + b_ref[...])

    r = gate(wra_ref, bra_ref)
    i = gate(wrx_ref, brx_ref)
    nl = -lam_ref[...]
    softplus = jnp.maximum(nl, 0.0) + jnp.log1p(jnp.exp(-jnp.abs(nl)))
    neg_log_a = RG_C * r * softplus
    a = jnp.exp(-neg_log_a)
    v = jnp.tanh(neg_log_a) * (a * a + 1.0)
    u = jnp.where(v > 0.0, v * lax.rsqrt(v), 0.0) * (i * xc)
    n_seg = V7X_SUBLANES
    seg = ts // n_seg
    pitch = a_s.shape[1] // n_seg
    slabs = w // V7X_LANES
    for l in range(slabs):
        lanes = slice(l * V7X_LANES, (l + 1) * V7X_LANES)
        for s in range(n_seg):
            a_s[l, s * pitch:s * pitch + seg, :] = a[s * seg:(s + 1) * seg, lanes]
            h_s[l, s * pitch:s * pitch + seg, :] = u[s * seg:(s + 1) * seg, lanes]

    def sweep(k, carry):
        h_loc, a_cum = carry
        h_new, a_new = [], []
        for l in range(slabs):
            rows = pl.ds(k, n_seg, stride=pitch)
            ak = a_s[l, rows, :]
            hk = ak * h_loc[l] + h_s[l, rows, :]
            pk = a_cum[l] * ak
            h_s[l, rows, :] = hk
            a_s[l, rows, :] = pk
            h_new.append(hk)
            a_new.append(pk)
        return h_new, a_new

    seg_shape = (n_seg, V7X_LANES)
    h_end, a_end = lax.fori_loop(
        0, seg, sweep,
        ([jnp.zeros(seg_shape, F32)] * slabs, [jnp.ones(seg_shape, F32)] * slabs), unroll=8)

    row = lax.broadcasted_iota(jnp.int32, seg_shape, 0)
    enter = []
    for l in range(slabs):
        ae, he = a_end[l], h_end[l]
        d = 1
        while d < n_seg:
            keep = row >= d
            a_prev = jnp.where(keep, pltpu.roll(ae, d, 0), 1.0)
            h_prev = jnp.where(keep, pltpu.roll(he, d, 0), 0.0)
            he = ae * h_prev + he
            ae = ae * a_prev
            d *= 2
        h_in = hcar[l]
        h_out = ae * h_in + he
        enter.append(jnp.where(row == 0, h_in, pltpu.roll(h_out, 1, 0)))
        hcar[l] = jnp.broadcast_to(h_out[n_seg - 1:n_seg, :], seg_shape)

    hr = jnp.concatenate(
        [jnp.concatenate([h_s[l, s * pitch:s * pitch + seg, :]
                          + a_s[l, s * pitch:s * pitch + seg, :] * enter[l][s:s + 1, :]
                          for l in range(slabs)], axis=1)
         for s in range(n_seg)], axis=0)
    pa = (jax.nn.gelu(ag_ref[...].astype(F32), approximate=True) * hr).astype(BF16)
    ya = _dot(pa, woa_ref[...])

    cv = _causal_conv(bcg_ref[...].astype(F32) * bv_ref[...].astype(F32), cbuf, cbw_ref, cbb_ref)
    pb = (bbg_ref[...].astype(F32) * cv).astype(BF16)
    yb = _dot(pb, wob_ref[...])

    for half, (ga_ref, gb_ref) in enumerate(((ga0_ref, gb0_ref), (ga1_ref, gb1_ref))):
        cols = slice(half * w, (half + 1) * w)
        o_ref[:, cols] = (jax.nn.sigmoid(ga_ref[...].astype(F32)) * ya[:, cols]
                          + jax.nn.sigmoid(gb_ref[...].astype(F32)) * yb[:, cols]).astype(o_ref.dtype)


def _mixer(z, p, w_out_a, w_out_b, layer, bsz, seq, ts=512):
    n = z.shape[0]
    w = p["conv_a_w"].shape[-1]
    d = w_out_a.shape[-1]
    assert p["conv_b_w"].shape[-1] == w and d == 2 * w and w % V7X_MXU_DIM == 0
    tiles = seq // ts
    seg = ts // V7X_SUBLANES
    seg_pitch = seg if (seg // V7X_SUBLANES) % 2 else seg + V7X_SUBLANES

    def zcol(c):
        return pl.BlockSpec((ts, w), lambda b, s: (b * tiles + s, c))

    def vec():
        return pl.BlockSpec((None, 1, w), lambda b, s: (layer, 0, 0))

    def full(shape):
        nd = len(shape)
        return pl.BlockSpec((None,) + shape, lambda b, s: (layer,) + (0,) * nd)

    def whole(shape):
        return pl.BlockSpec(shape, lambda b, s: (0,) * len(shape))

    groups = w // V7X_MXU_DIM
    return pl.pallas_call(
        _mixer_kernel,
        grid=(bsz, tiles),
        in_specs=[zcol(c) for c in range(9)] + [
            full((RG_CONV, w)), vec(),
            full((groups, V7X_MXU_DIM, V7X_MXU_DIM)), vec(),
            full((groups, V7X_MXU_DIM, V7X_MXU_DIM)), vec(),
            vec(), whole((w, d)),
            full((SC_WIDTH, w)), vec(), whole((w, d)),
        ],
        out_specs=pl.BlockSpec((ts, d), lambda b, s: (b * tiles + s, 0)),
        out_shape=jax.ShapeDtypeStruct((n, d), BF16),
        scratch_shapes=[
            pltpu.VMEM((V7X_SUBLANES, w), F32),
            pltpu.VMEM((V7X_SUBLANES, w), F32),
            pltpu.VMEM((w // V7X_LANES, V7X_SUBLANES * seg_pitch, V7X_LANES), F32),
            pltpu.VMEM((w // V7X_LANES, V7X_SUBLANES * seg_pitch, V7X_LANES), F32),
            pltpu.VMEM((w // V7X_LANES, V7X_SUBLANES, V7X_LANES), F32),
        ],
        compiler_params=_params(("arbitrary", "arbitrary"), 56),
        name="mixer",
    )(*([z] * 9), p["conv_a_w"], p["conv_a_b"], p["w_rg_a"], p["b_rg_a"], p["w_rg_x"], p["b_rg_x"],
      p["rg_lambda"], w_out_a, p["conv_b_w"], p["conv_b_b"], w_out_b)


ROUTE_FIRST, ROUTE_SECOND, ROUTE_W_FIRST, ROUTE_W_SECOND = 0, 1, 2, 3


def _top2_route(logits):
    lane = lax.broadcasted_iota(jnp.int32, logits.shape, 1).astype(F32)
    n_lanes = float(logits.shape[-1])
    v1 = jnp.max(logits, axis=-1, keepdims=True)
    i1 = jnp.min(jnp.where(logits == v1, lane, n_lanes), axis=-1, keepdims=True)
    rest = jnp.where(lane == i1, -jnp.inf, logits)
    v2 = jnp.max(rest, axis=-1, keepdims=True)
    i2 = jnp.min(jnp.where(rest == v2, lane, n_lanes), axis=-1, keepdims=True)
    e = jnp.exp(v2 - v1)
    w1 = 1.0 / (1.0 + e)
    w2 = e / (1.0 + e)
    out = jnp.where(lane == ROUTE_FIRST, i1, 0.0)
    out = jnp.where(lane == ROUTE_SECOND, i2, out)
    out = jnp.where(lane == ROUTE_W_FIRST, w1, out)
    return jnp.where(lane == ROUTE_W_SECOND, w2, out)


def _out_proj_kernel(*refs, with_router):
    if with_router:
        m_ref, wo_ref, x_ref, mod_ref, g_ref, wr_ref, br_ref, xo_ref, ho_ref, route_ref = refs
    else:
        m_ref, wo_ref, x_ref, mod_ref, g_ref, xo_ref, ho_ref = refs
    mod = mod_ref[...]
    chunk = m_ref.shape[0] // 2
    for rows in (pl.ds(0, chunk), pl.ds(chunk, chunk)):
        xn = x_ref[rows, :] + mod[2:3] * _dot(m_ref[rows, :], wo_ref[...])
        xo_ref[rows, :] = xn
        h2 = _rms_mod(xn, g_ref[...], mod[3:4], mod[4:5])
        ho_ref[rows, :] = h2.astype(ho_ref.dtype)
        if with_router:
            route_ref[rows, :] = _top2_route(_dot(h2.astype(BF16), wr_ref[...]) + br_ref[...])


def _out_proj(merged, w_o, x, mod, norm_g, layer, seq, router=None, tm=512):
    n, d = x.shape
    tiles_per_seq = seq // tm
    in_specs = [
        pl.BlockSpec((tm, d), lambda i: (i, 0)),
        pl.BlockSpec((d, d), lambda i: (0, 0)),
        pl.BlockSpec((tm, d), lambda i: (i, 0)),
        pl.BlockSpec((None, None, N_MOD, d), lambda i: (layer, i // tiles_per_seq, 0, 0)),
        pl.BlockSpec((None, 1, d), lambda i: (layer, 0, 0)),
    ]
    out_specs = [pl.BlockSpec((tm, d), lambda i: (i, 0)), pl.BlockSpec((tm, d), lambda i: (i, 0))]
    out_shape = [jax.ShapeDtypeStruct((n, d), F32),
                 jax.ShapeDtypeStruct((n, d), BF16 if router is None else F32)]
    args = [merged, w_o, x, mod, norm_g]
    if router is not None:
        w_r, b_r, j = router
        in_specs += [pl.BlockSpec((None, d, V7X_LANES), lambda i: (j, 0, 0)),
                     pl.BlockSpec((None, 1, V7X_LANES), lambda i: (j, 0, 0))]
        out_specs.append(pl.BlockSpec((tm, V7X_LANES), lambda i: (i, 0)))
        out_shape.append(jax.ShapeDtypeStruct((n, V7X_LANES), F32))
        args += [w_r, b_r]
    return pl.pallas_call(
        functools.partial(_out_proj_kernel, with_router=router is not None),
        grid=(n // tm,),
        in_specs=in_specs, out_specs=out_specs, out_shape=out_shape,
        input_output_aliases={2: 0} if layer > 0 else {},
        compiler_params=_params(("arbitrary",), 56),
        name="out_proj",
    )(*args)


def _residual_norm(x, gate, f, g, nmod, last):
    xn = x + gate * f
    if last:
        zero = jnp.zeros_like(g)
        return xn, _rms_mod(xn, g, zero, zero)
    return xn, _rms_mod(xn, g, nmod[0:1], nmod[1:2])


def _swiglu(h, wg_ref, wu_ref, wd_ref):
    a = _dot(h, wg_ref[...])
    act = (a * jax.nn.sigmoid(a)) * _dot(h, wu_ref[...])
    return _dot(act.astype(BF16), wd_ref[...])


def _ffn_kernel(h_ref, wg_ref, wu_ref, wd_ref, x_ref, mod_ref, g_ref, nmod_ref, xo_ref, ho_ref, *, last):
    j = pl.program_id(1)
    n_j = pl.num_programs(1)

    @pl.when(j == 0)
    def _():
        xo_ref[...] = _swiglu(h_ref[...], wg_ref, wu_ref, wd_ref)

    @pl.when(jnp.logical_and(j > 0, j < n_j - 1))
    def _():
        xo_ref[...] += _swiglu(h_ref[...], wg_ref, wu_ref, wd_ref)

    @pl.when(j == n_j - 1)
    def _():
        f = xo_ref[...] + _swiglu(h_ref[...], wg_ref, wu_ref, wd_ref)
        xn, hn = _residual_norm(x_ref[...], mod_ref[...][5:6], f, g_ref[...], nmod_ref[...], last)
        xo_ref[...] = xn
        ho_ref[...] = hn.astype(ho_ref.dtype)


def _ffn(h, wg, wu, wd, w_index, x, mod, layer, next_g, next_g_index, next_layer, seq,
         last=False, tm=512, tf=1024):
    n, d = x.shape
    f = wg.shape[-1]
    tiles_per_seq = seq // tm
    return pl.pallas_call(
        functools.partial(_ffn_kernel, last=last),
        grid=(n // tm, f // tf),
        in_specs=[
            pl.BlockSpec((tm, d), lambda i, j: (i, 0)),
            pl.BlockSpec((None, d, tf), lambda i, j: (w_index, 0, j)),
            pl.BlockSpec((None, d, tf), lambda i, j: (w_index, 0, j)),
            pl.BlockSpec((None, tf, d), lambda i, j: (w_index, j, 0)),
            pl.BlockSpec((tm, d), lambda i, j: (i, 0)),
            pl.BlockSpec((None, None, N_MOD, d), lambda i, j: (layer, i // tiles_per_seq, 0, 0)),
            pl.BlockSpec((None, 1, d), lambda i, j: (next_g_index, 0, 0)),
            pl.BlockSpec((None, None, N_MOD, d), lambda i, j: (next_layer, i // tiles_per_seq, 0, 0)),
        ],
        out_specs=[pl.BlockSpec((tm, d), lambda i, j: (i, 0)),
                   pl.BlockSpec((tm, d), lambda i, j: (i, 0))],
        out_shape=[jax.ShapeDtypeStruct((n, d), F32),
                   jax.ShapeDtypeStruct((n, d), F32 if last else BF16)],
        input_output_aliases={4: 0},
        compiler_params=_params(("arbitrary", "arbitrary"), 58),
        name="ffn",
    )(h, wg, wu, wd, x, mod, next_g, mod)


TOP_K = 2
ROW_DMA_UNROLL = 8


def _route_plan(route, n_experts, tm):
    n = route.shape[0]
    n_tiles = TOP_K * n // tm + n_experts - 1
    choice = route[:, ROUTE_FIRST:ROUTE_SECOND + 1].astype(jnp.int32)
    member = jnp.sum(choice[:, :, None] == jnp.arange(n_experts)[None, None, :], axis=1,
                     dtype=jnp.int32)
    rank = jnp.cumsum(member, axis=0) - member
    counts = rank[-1] + member[-1]
    padded = (counts + tm - 1) // tm * tm
    ends = jnp.cumsum(padded)
    dest = jnp.take_along_axis((ends - padded)[None, :] + rank, choice, axis=1)
    n_used = ends[-1] // tm
    tile = jnp.minimum(jnp.arange(n_tiles), n_used - 1)
    tile_expert = jnp.sum(tile[:, None] * tm >= ends[None, :], axis=1, dtype=jnp.int32)
    tile_rows = jnp.clip(jnp.take(ends - padded + counts, tile_expert) - tile * tm, 0, tm)
    n_rows = n_tiles * tm
    gap_len = jnp.concatenate([padded - counts, n_rows - ends[-1:]])
    gap_start = jnp.concatenate([ends - (padded - counts), ends[-1:]])
    gap_cum = jnp.cumsum(gap_len)
    k = jnp.arange(n_rows - TOP_K * n)
    gap = jnp.sum(k[:, None] >= gap_cum[None, :], axis=1)
    free_rows = jnp.take(gap_start - (gap_cum - gap_len), gap) + k
    return (dest.reshape(-1).astype(jnp.int32), free_rows.astype(jnp.int32), tile_expert,
            tile_rows.astype(jnp.int32), n_used.reshape(1).astype(jnp.int32), n_tiles)


def _dispatch_kernel(dest_ref, free_ref, h_ref, o_ref, zrow, sem, *, free_per_step):
    tt = h_ref.shape[0]
    step = pl.program_id(0)
    base = step * tt

    @pl.when(step == 0)
    def _():
        zrow[...] = jnp.zeros_like(zrow)

    def row_copy(r, k):
        row = dest_ref[TOP_K * (base + r) + k]
        return pltpu.make_async_copy(h_ref.at[pl.ds(r, 1), :], o_ref.at[pl.ds(row, 1), :], sem)

    def zero_copy(q):
        row = free_ref[step * free_per_step + q]
        return pltpu.make_async_copy(zrow.at[pl.ds(0, 1), :], o_ref.at[pl.ds(row, 1), :], sem)

    @pl.loop(0, tt, unroll=ROW_DMA_UNROLL)
    def _(r):
        for k in range(TOP_K):
            row_copy(r, k).start(priority=k)

    @pl.loop(0, free_per_step, unroll=ROW_DMA_UNROLL)
    def _(q):
        zero_copy(q).start()

    for _ in range(TOP_K):
        pltpu.make_async_copy(h_ref, o_ref.at[pl.ds(0, tt), :], sem).wait()
    pltpu.make_async_copy(h_ref.at[pl.ds(0, free_per_step), :],
                          o_ref.at[pl.ds(0, free_per_step), :], sem).wait()


def _dispatch(h, dest, free_rows, n_rows, tt=512):
    n, d = h.shape
    steps = n // tt
    assert free_rows.shape[0] == n_rows - TOP_K * n and free_rows.shape[0] % (steps * ROW_DMA_UNROLL) == 0
    return pl.pallas_call(
        functools.partial(_dispatch_kernel, free_per_step=free_rows.shape[0] // steps),
        grid_spec=pltpu.PrefetchScalarGridSpec(
            num_scalar_prefetch=2,
            grid=(steps,),
            in_specs=[pl.BlockSpec((tt, d), lambda i, dest, free: (i, 0))],
            out_specs=pl.BlockSpec(memory_space=pl.ANY),
            scratch_shapes=[pltpu.VMEM((V7X_SUBLANES, d), h.dtype), pltpu.SemaphoreType.DMA(())],
        ),
        out_shape=jax.ShapeDtypeStruct((n_rows, d), h.dtype),
        compiler_params=_params(("arbitrary",), 32),
        name="dispatch",
    )(dest, free_rows, h)


def _expert_kernel(te_ref, tr_ref, nu_ref, x_ref, wg_ref, wu_ref, wd_ref, y_ref, xb):
    del te_ref
    i = pl.program_id(0)
    j = pl.program_id(1)
    used = i < nu_ref[0]
    half = y_ref.shape[0] // 2
    partial = tr_ref[i] <= half

    @pl.when(jnp.logical_and(jnp.logical_not(used), j == 0))
    def _():
        y_ref[...] = jnp.zeros_like(y_ref)

    def step(rows, first):
        if first:
            h = x_ref[rows, :].astype(BF16)
            xb[rows, :] = h
            y_ref[rows, :] = _swiglu(h, wg_ref, wu_ref, wd_ref)
        else:
            y_ref[rows, :] += _swiglu(xb[rows, :], wg_ref, wu_ref, wd_ref)

    whole, head, tail = slice(None), slice(0, half), slice(half, 2 * half)
    for first in (True, False):
        at_step = (j == 0) if first else (j > 0)

        @pl.when(jnp.logical_and(jnp.logical_and(used, at_step), jnp.logical_not(partial)))
        def _():
            step(whole, first)

        @pl.when(jnp.logical_and(jnp.logical_and(used, at_step), partial))
        def _():
            step(head, first)
            if first:
                y_ref[tail, :] = jnp.zeros((half, y_ref.shape[1]), y_ref.dtype)


def _expert_ffn(xs, wg, wu, wd, w_base, tile_expert, tile_rows, n_used, n_tiles, tm, tf=1024):
    d = xs.shape[-1]
    f = wg.shape[-1]
    n_j = f // tf

    def row(i, j, te, tr, nu):
        return (jnp.minimum(i, nu[0] - 1), 0)

    def col(i, j, nu):
        return jnp.where(i < nu[0], j, n_j - 1)

    return pl.pallas_call(
        _expert_kernel,
        grid_spec=pltpu.PrefetchScalarGridSpec(
            num_scalar_prefetch=3,
            grid=(n_tiles, n_j),
            in_specs=[
                pl.BlockSpec((tm, d), row),
                pl.BlockSpec((None, d, tf), lambda i, j, te, tr, nu: (w_base + te[i], 0, col(i, j, nu))),
                pl.BlockSpec((None, d, tf), lambda i, j, te, tr, nu: (w_base + te[i], 0, col(i, j, nu))),
                pl.BlockSpec((None, tf, d), lambda i, j, te, tr, nu: (w_base + te[i], col(i, j, nu), 0)),
            ],
            out_specs=pl.BlockSpec((tm, d), lambda i, j, te, tr, nu: (i, 0)),
            scratch_shapes=[pltpu.VMEM((tm, d), BF16)],
        ),
        out_shape=jax.ShapeDtypeStruct((n_tiles * tm, d), F32),
        compiler_params=_params(("arbitrary", "arbitrary"), 56),
        name="expert_ffn",
    )(tile_expert, tile_rows, n_used, xs, wg, wu, wd)


def _combine_kernel(dest_ref, y_ref, x_ref, route_ref, mod_ref, g_ref, nmod_ref, *rest, last):
    if last:
        ho_ref, ybuf, sem = rest
    else:
        xo_ref, ho_ref, ybuf, sem = rest
    tc = x_ref.shape[0]
    step = pl.program_id(0)
    slot = step % 2

    def request_rows(s, into):
        @pl.loop(0, tc, unroll=ROW_DMA_UNROLL)
        def _(r):
            for k in range(TOP_K):
                row = dest_ref[TOP_K * (s * tc + r) + k]
                pltpu.make_async_copy(y_ref.at[pl.ds(row, 1), :], ybuf.at[into, k, pl.ds(r, 1), :],
                                      sem.at[into, k]).start(priority=k)

    @pl.when(step == 0)
    def _():
        request_rows(0, 0)

    @pl.when(step + 1 < pl.num_programs(0))
    def _():
        request_rows(step + 1, 1 - slot)

    for k in range(TOP_K):
        pltpu.make_async_copy(y_ref.at[pl.ds(0, tc), :], ybuf.at[slot, k], sem.at[slot, k]).wait()

    route = route_ref[...]
    f = (route[:, ROUTE_W_FIRST:ROUTE_W_FIRST + 1] * ybuf[slot, 0]
         + route[:, ROUTE_W_SECOND:ROUTE_W_SECOND + 1] * ybuf[slot, 1])
    xn, hn = _residual_norm(x_ref[...], mod_ref[...][5:6], f, g_ref[...], nmod_ref[...], last)
    if not last:
        xo_ref[...] = xn
    ho_ref[...] = hn.astype(ho_ref.dtype)


def _combine(y, dest, x, route, mod, layer, next_g, next_g_index, next_layer, seq, last, tc=512):
    n, d = x.shape
    tiles_per_seq = seq // tc
    tok = pl.BlockSpec((tc, d), lambda i, dest: (i, 0))
    out_specs = [tok] if last else [tok, tok]
    out_shape = ([jax.ShapeDtypeStruct((n, d), F32)] if last else
                 [jax.ShapeDtypeStruct((n, d), F32), jax.ShapeDtypeStruct((n, d), BF16)])
    return pl.pallas_call(
        functools.partial(_combine_kernel, last=last),
        grid_spec=pltpu.PrefetchScalarGridSpec(
            num_scalar_prefetch=1,
            grid=(n // tc,),
            in_specs=[
                pl.BlockSpec(memory_space=pl.ANY),
                tok,
                pl.BlockSpec((tc, V7X_LANES), lambda i, dest: (i, 0)),
                pl.BlockSpec((None, None, N_MOD, d), lambda i, dest: (layer, i // tiles_per_seq, 0, 0)),
                pl.BlockSpec((None, 1, d), lambda i, dest: (next_g_index, 0, 0)),
                pl.BlockSpec((None, None, N_MOD, d), lambda i, dest: (next_layer, i // tiles_per_seq, 0, 0)),
            ],
            out_specs=out_specs,
            scratch_shapes=[pltpu.VMEM((2, TOP_K, tc, d), F32), pltpu.SemaphoreType.DMA((2, TOP_K))],
        ),
        out_shape=out_shape,
        input_output_aliases={} if last else {2: 0},
        compiler_params=_params(("arbitrary",), 48),
        name="combine",
    )(dest, y, x, route, mod, next_g, mod)


def _moe(h2, route, wg, wu, wd, w_base, n_experts, x, mod, layer, next_g, next_g_index, next_layer,
         seq, last, tm=512):
    dest, free_rows, tile_expert, tile_rows, n_used, n_tiles = _route_plan(route, n_experts, tm)
    xs = _dispatch(h2, dest, free_rows, n_tiles * tm)
    y = _expert_ffn(xs, wg, wu, wd, w_base, tile_expert, tile_rows, n_used, n_tiles, tm)
    return _combine(y, dest, x, route, mod, layer, next_g, next_g_index, next_layer, seq, last)


def _block_diag_groups(w_heads):
    depth, heads, hd, _ = w_heads.shape
    per = V7X_MXU_DIM // hd
    groups = heads // per
    wg = w_heads.reshape(depth, groups, per, hd, hd)
    eye = jnp.eye(per, dtype=w_heads.dtype)
    out = jnp.einsum("lgpde,pq->lgpdqe", wg, eye)
    return out.reshape(depth, groups, V7X_MXU_DIM, V7X_MXU_DIM)


def kernel(x, c, w_mod, b_mod, norm1_g, norm2_g, w_in, conv_a_w, conv_a_b, w_rg_a, b_rg_a, w_rg_x, b_rg_x, rg_lambda, w_out_a, conv_b_w, conv_b_b, w_out_b, w_o, w_ff_gate, w_ff_up, w_ff_down, w_router, b_router, w_e_gate, w_e_up, w_e_down, final_g):
    bsz, seq, d = x.shape
    depth = w_in.shape[0]
    n = bsz * seq
    n_moe, n_experts = w_router.shape[0], w_router.shape[-1]
    w_rnn = conv_a_w.shape[-1]

    mixer_p = dict(
        conv_a_w=conv_a_w, conv_a_b=conv_a_b.reshape(depth, 1, w_rnn),
        w_rg_a=_block_diag_groups(w_rg_a).astype(BF16), b_rg_a=b_rg_a.reshape(depth, 1, w_rnn),
        w_rg_x=_block_diag_groups(w_rg_x).astype(BF16), b_rg_x=b_rg_x.reshape(depth, 1, w_rnn),
        rg_lambda=rg_lambda.reshape(depth, 1, w_rnn),
        conv_b_w=conv_b_w, conv_b_b=conv_b_b.reshape(depth, 1, w_rnn),
    )
    f_d, f_e = w_ff_gate.shape[-1], w_e_gate.shape[-1]
    dense_w = (w_ff_gate, w_ff_up, w_ff_down)
    expert_w = (w_e_gate.reshape(n_moe, n_experts * d, f_e), w_e_up.reshape(n_moe, n_experts * d, f_e),
                w_e_down.reshape(n_moe, n_experts * f_e, d))
    w_r = jnp.pad(w_router, ((0, 0), (0, 0), (0, V7X_LANES - n_experts))).astype(BF16)
    b_r = jnp.pad(b_router, ((0, 0), (0, V7X_LANES - n_experts)),
                  constant_values=ROUTER_PAD_BIAS).reshape(n_moe, 1, V7X_LANES)
    norm1 = norm1_g.reshape(depth, 1, d)
    norm2 = norm2_g.reshape(depth, 1, d)
    final = final_g.reshape(1, 1, d)

    mod = _modulation(c, w_mod, b_mod).reshape(depth, bsz, N_MOD, d)
    xs = x.reshape(n, d)
    h = _first_norm(xs, mod, norm1, seq)
    for l in range(depth):
        is_last = l == depth - 1
        nxt = (final, 0, l) if is_last else (norm1, l + 1, l + 1)
        j = l // 2
        layer_w = [(w_out_a, l), (w_out_b, l), (w_o, l)]
        if l % 2 == 0:
            z, woa, wob, wo, wg, wu, wd = _in_proj(h, w_in, l, layer_w + [(w, j) for w in dense_w])
            merged = _mixer(z, mixer_p, woa, wob, l, bsz, seq)
            xs, h2 = _out_proj(merged, wo, xs, mod, norm2, l, seq)
            xs, h = _ffn(h2, wg.reshape(1, d, f_d), wu.reshape(1, d, f_d), wd.reshape(1, f_d, d), 0,
                         xs, mod, l, *nxt, seq, last=is_last)
        else:
            z, woa, wob, wo, wg, wu, wd = _in_proj(h, w_in, l, layer_w + [(w, j) for w in expert_w])
            merged = _mixer(z, mixer_p, woa, wob, l, bsz, seq)
            xs, h2, route = _out_proj(merged, wo, xs, mod, norm2, l, seq, router=(w_r, b_r, j))
            outs = _moe(h2, route, wg.reshape(n_experts, d, f_e), wu.reshape(n_experts, d, f_e),
                        wd.reshape(n_experts, f_e, d), 0, n_experts, xs, mod, l, *nxt, seq,
                        is_last)
            if is_last:
                (h,) = outs
            else:
                xs, h = outs
    return h.reshape(bsz, seq, d)
```

```python
import functools

import jax
import jax.numpy as jnp
from jax import lax
from jax.experimental import pallas as pl
from jax.experimental.pallas import tpu as pltpu

F32 = jnp.float32
BF16 = jnp.bfloat16

V7X_LANES = 128
V7X_SUBLANES = 8
V7X_MXU_DIM = 256
V7X_VMEM_BYTES = 64 * 1024 * 1024

RG_C = 8.0
RG_CONV = 4
SC_WIDTH = 3
N_MOD = 6
EPS = 1e-6
ROUTER_PAD_BIAS = -1e30


def _params(semantics, vmem_mib):
    assert vmem_mib * 1024 * 1024 <= V7X_VMEM_BYTES
    return pltpu.CompilerParams(dimension_semantics=semantics,
                                vmem_limit_bytes=vmem_mib * 1024 * 1024)


def _dot(a, b):
    return jnp.dot(a, b, preferred_element_type=F32)


def _rms_mod(x, g, shift, scale):
    ms = jnp.mean(x * x, axis=-1, keepdims=True)
    y = x * lax.rsqrt(ms + EPS) * g
    return y * (1.0 + scale) + shift


def _mod_kernel(c_ref, w_ref, b_ref, o_ref):
    c = c_ref[...]
    c_act = (c * jax.nn.sigmoid(c)).astype(BF16)
    o_ref[...] = _dot(c_act, w_ref[...].astype(BF16)) + b_ref[...]


def _modulation(c, w_mod, b_mod, tn=2048):
    depth, d, n = w_mod.shape
    bsz = c.shape[0]
    return pl.pallas_call(
        _mod_kernel,
        grid=(depth, n // tn),
        in_specs=[
            pl.BlockSpec((bsz, d), lambda l, j: (0, 0)),
            pl.BlockSpec((None, d, tn), lambda l, j: (l, 0, j)),
            pl.BlockSpec((None, 1, tn), lambda l, j: (l, 0, j)),
        ],
        out_specs=pl.BlockSpec((None, bsz, tn), lambda l, j: (l, 0, j)),
        out_shape=jax.ShapeDtypeStruct((depth, bsz, n), F32),
        compiler_params=_params(("arbitrary", "arbitrary"), 40),
        name="modulation",
    )(c, w_mod, b_mod.reshape(depth, 1, n))


def _norm_kernel(x_ref, mod_ref, g_ref, o_ref):
    m = mod_ref[...]
    o_ref[...] = _rms_mod(x_ref[...], g_ref[...], m[0:1], m[1:2]).astype(o_ref.dtype)


def _first_norm(x, mod, norm_g, seq, tm=1024):
    n, d = x.shape
    tiles_per_seq = seq // tm
    return pl.pallas_call(
        _norm_kernel,
        grid=(n // tm,),
        in_specs=[
            pl.BlockSpec((tm, d), lambda i: (i, 0)),
            pl.BlockSpec((None, None, N_MOD, d), lambda i: (0, i // tiles_per_seq, 0, 0)),
            pl.BlockSpec((None, 1, d), lambda i: (0, 0, 0)),
        ],
        out_specs=pl.BlockSpec((tm, d), lambda i: (i, 0)),
        out_shape=jax.ShapeDtypeStruct((n, d), BF16),
        compiler_params=_params(("arbitrary",), 32),
        name="first_norm",
    )(x, mod, norm_g)


ROUND_CHUNKS = 128


def _in_proj_kernel(h_ref, w_ref, *rest, n_side):
    side_in, o_ref, side_out, wb = rest[:n_side], rest[n_side], rest[n_side + 1:-1], rest[-1]

    @pl.when(pl.program_id(1) == 0)
    def _():
        wb[...] = w_ref[...].astype(BF16)

    o_ref[...] = _dot(h_ref[...], wb[...]).astype(o_ref.dtype)

    for src, dst in zip(side_in, side_out):
        dst[...] = src[...].astype(BF16)


def _in_proj(h, w_in, layer, side, tm=1024, tn=1024):
    n, d = h.shape
    d_in = w_in.shape[-1]
    grid = (d_in // tn, n // tm)
    assert grid[0] * grid[1] >= ROUND_CHUNKS
    bf16_rows = 2 * V7X_SUBLANES

    side_in, side_out, side_shape = [], [], []
    for w, index in side:
        _, rows, cols = w.shape
        n_chunks = min(ROUND_CHUNKS, rows // bf16_rows)
        chunk_rows = rows // n_chunks
        assert chunk_rows * n_chunks == rows and chunk_rows % bf16_rows == 0

        def chunk(j, i, n_chunks=n_chunks):
            return jnp.minimum(j * grid[1] + i, n_chunks - 1)

        side_in.append(pl.BlockSpec((None, chunk_rows, cols),
                                    lambda j, i, index=index, chunk=chunk: (index, chunk(j, i), 0)))
        side_out.append(pl.BlockSpec((chunk_rows, cols), lambda j, i, chunk=chunk: (chunk(j, i), 0)))
        side_shape.append(jax.ShapeDtypeStruct((rows, cols), BF16))
    return pl.pallas_call(
        functools.partial(_in_proj_kernel, n_side=len(side)),
        grid=grid,
        in_specs=[
            pl.BlockSpec((tm, d), lambda j, i: (i, 0)),
            pl.BlockSpec((None, d, tn), lambda j, i: (layer, 0, j)),
        ] + side_in,
        out_specs=[pl.BlockSpec((tm, tn), lambda j, i: (i, j))] + side_out,
        out_shape=[jax.ShapeDtypeStruct((n, d_in), BF16)] + side_shape,
        scratch_shapes=[pltpu.VMEM((d, tn), BF16)],
        compiler_params=_params(("arbitrary", "arbitrary"), 56),
        name="in_proj",
    )(h, w_in, *[w for w, _ in side])


def _causal_conv(x, tail_ref, w_ref, b_ref):
    ts, w = x.shape
    taps = w_ref.shape[0]
    row = lax.broadcasted_iota(jnp.int32, (V7X_SUBLANES, w), 0)
    prev = tail_ref[...]
    y = b_ref[...] + x * w_ref[taps - 1:taps, :]
    for d in range(1, taps):
        shifted = pltpu.roll(x, d, 0)
        head = jnp.where(row < d, pltpu.roll(prev, d, 0), shifted[0:V7X_SUBLANES, :])
        shifted = jnp.concatenate([head, shifted[V7X_SUBLANES:, :]], axis=0)
        y = y + shifted * w_ref[taps - 1 - d:taps - d, :]
    tail_ref[...] = x[ts - V7X_SUBLANES:ts, :]
    return y


def _mixer_kernel(ax_ref, ag_ref, bv_ref, bbg_ref, bcg_ref, ga0_ref, ga1_ref, gb0_ref, gb1_ref,
                  caw_ref, cab_ref, wra_ref, bra_ref, wrx_ref, brx_ref, lam_ref, woa_ref,
                  cbw_ref, cbb_ref, wob_ref, o_ref,
                  xbuf, cbuf, a_s, h_s, hcar):
    ts, w = ax_ref.shape
    groups = w // V7X_MXU_DIM

    @pl.when(pl.program_id(1) == 0)
    def _():
        xbuf[...] = jnp.zeros_like(xbuf)
        cbuf[...] = jnp.zeros_like(cbuf)
        hcar[...] = jnp.zeros_like(hcar)

    xc = _causal_conv(ax_ref[...].astype(F32), xbuf, caw_ref, cab_ref)

    xcb = xc.astype(BF16)

    def gate(w_ref, b_ref):
        parts = [_dot(xcb[:, g * V7X_MXU_DIM:(g + 1) * V7X_MXU_DIM], w_ref[g])
                 for g in range(groups)]
        return jax.nn.sigmoid(jnp.concatenate(parts, axis=-1) + b_ref[...])

    r = gate(wra_ref, bra_ref)
    i = gate(wrx_ref, brx_ref)
    nl = -lam_ref[...]
    softplus = jnp.maximum(nl, 0.0) + jnp.log1p(jnp.exp(-jnp.abs(nl)))
    neg_log_a = RG_C * r * softplus
    a = jnp.exp(-neg_log_a)
    v = jnp.tanh(neg_log_a) * (a * a + 1.0)
    u = jnp.where(v > 0.0, v * lax.rsqrt(v), 0.0) * (i * xc)
    n_seg = V7X_SUBLANES
    seg = ts // n_seg
    pitch = a_s.shape[1] // n_seg
    slabs = w // V7X_LANES
    for l in range(slabs):
        lanes = slice(l * V7X_LANES, (l + 1) * V7X_LANES)
        for s in range(n_seg):
            a_s[l, s * pitch:s * pitch + seg, :] = a[s * seg:(s + 1) * seg, lanes]
            h_s[l, s * pitch:s * pitch + seg, :] = u[s * seg:(s + 1) * seg, lanes]

    def sweep(k, carry):
        h_loc, a_cum = carry
        h_new, a_new = [], []
        for l in range(slabs):
            rows = pl.ds(k, n_seg, stride=pitch)
            ak = a_s[l, rows, :]
            hk = ak * h_loc[l] + h_s[l, rows, :]
            pk = a_cum[l] * ak
            h_s[l, rows, :] = hk
            a_s[l, rows, :] = pk
            h_new.append(hk)
            a_new.append(pk)
        return h_new, a_new

    seg_shape = (n_seg, V7X_LANES)
    h_end, a_end = lax.fori_loop(
        0, seg, sweep,
        ([jnp.zeros(seg_shape, F32)] * slabs, [jnp.ones(seg_shape, F32)] * slabs), unroll=8)

    row = lax.broadcasted_iota(jnp.int32, seg_shape, 0)
    enter = []
    for l in range(slabs):
        ae, he = a_end[l], h_end[l]
        d = 1
        while d < n_seg:
            keep = row >= d
            a_prev = jnp.where(keep, pltpu.roll(ae, d, 0), 1.0)
            h_prev = jnp.where(keep, pltpu.roll(he, d, 0), 0.0)
            he = ae * h_prev + he
            ae = ae * a_prev
            d *= 2
        h_in = hcar[l]
        h_out = ae * h_in + he
        enter.append(jnp.where(row == 0, h_in, pltpu.roll(h_out, 1, 0)))
        hcar[l] = jnp.broadcast_to(h_out[n_seg - 1:n_seg, :], seg_shape)

    hr = jnp.concatenate(
        [jnp.concatenate([h_s[l, s * pitch:s * pitch + seg, :]
                          + a_s[l, s * pitch:s * pitch + seg, :] * enter[l][s:s + 1, :]
                          for l in range(slabs)], axis=1)
         for s in range(n_seg)], axis=0)
    pa = (jax.nn.gelu(ag_ref[...].astype(F32), approximate=True) * hr).astype(BF16)
    ya = _dot(pa, woa_ref[...])

    cv = _causal_conv(bcg_ref[...].astype(F32) * bv_ref[...].astype(F32), cbuf, cbw_ref, cbb_ref)
    pb = (bbg_ref[...].astype(F32) * cv).astype(BF16)
    yb = _dot(pb, wob_ref[...])

    for half, (ga_ref, gb_ref) in enumerate(((ga0_ref, gb0_ref), (ga1_ref, gb1_ref))):
        cols = slice(half * w, (half + 1) * w)
        o_ref[:, cols] = (jax.nn.sigmoid(ga_ref[...].astype(F32)) * ya[:, cols]
                          + jax.nn.sigmoid(gb_ref[...].astype(F32)) * yb[:, cols]).astype(o_ref.dtype)


def _mixer(z, p, w_out_a, w_out_b, layer, bsz, seq, ts=512):
    n = z.shape[0]
    w = p["conv_a_w"].shape[-1]
    d = w_out_a.shape[-1]
    assert p["conv_b_w"].shape[-1] == w and d == 2 * w and w % V7X_MXU_DIM == 0
    tiles = seq // ts
    seg = ts // V7X_SUBLANES
    seg_pitch = seg if (seg // V7X_SUBLANES) % 2 else seg + V7X_SUBLANES

    def zcol(c):
        return pl.BlockSpec((ts, w), lambda b, s: (b * tiles + s, c))

    def vec():
        return pl.BlockSpec((None, 1, w), lambda b, s: (layer, 0, 0))

    def full(shape):
        nd = len(shape)
        return pl.BlockSpec((None,) + shape, lambda b, s: (layer,) + (0,) * nd)

    def whole(shape):
        return pl.BlockSpec(shape, lambda b, s: (0,) * len(shape))

    groups = w // V7X_MXU_DIM
    return pl.pallas_call(
        _mixer_kernel,
        grid=(bsz, tiles),
        in_specs=[zcol(c) for c in range(9)] + [
            full((RG_CONV, w)), vec(),
            full((groups, V7X_MXU_DIM, V7X_MXU_DIM)), vec(),
            full((groups, V7X_MXU_DIM, V7X_MXU_DIM)), vec(),
            vec(), whole((w, d)),
            full((SC_WIDTH, w)), vec(), whole((w, d)),
        ],
        out_specs=pl.BlockSpec((ts, d), lambda b, s: (b * tiles + s, 0)),
        out_shape=jax.ShapeDtypeStruct((n, d), BF16),
        scratch_shapes=[
            pltpu.VMEM((V7X_SUBLANES, w), F32),
            pltpu.VMEM((V7X_SUBLANES, w), F32),
            pltpu.VMEM((w // V7X_LANES, V7X_SUBLANES * seg_pitch, V7X_LANES), F32),
            pltpu.VMEM((w // V7X_LANES, V7X_SUBLANES * seg_pitch, V7X_LANES), F32),
            pltpu.VMEM((w // V7X_LANES, V7X_SUBLANES, V7X_LANES), F32),
        ],
        compiler_params=_params(("arbitrary", "arbitrary"), 56),
        name="mixer",
    )(*([z] * 9), p["conv_a_w"], p["conv_a_b"], p["w_rg_a"], p["b_rg_a"], p["w_rg_x"], p["b_rg_x"],
      p["rg_lambda"], w_out_a, p["conv_b_w"], p["conv_b_b"], w_out_b)


ROUTE_FIRST, ROUTE_SECOND, ROUTE_W_FIRST, ROUTE_W_SECOND = 0, 1, 2, 3


def _top2_route(logits):
    lane = lax.broadcasted_iota(jnp.int32, logits.shape, 1).astype(F32)
    n_lanes = float(logits.shape[-1])
    v1 = jnp.max(logits, axis=-1, keepdims=True)
    i1 = jnp.min(jnp.where(logits == v1, lane, n_lanes), axis=-1, keepdims=True)
    rest = jnp.where(lane == i1, -jnp.inf, logits)
    v2 = jnp.max(rest, axis=-1, keepdims=True)
    i2 = jnp.min(jnp.where(rest == v2, lane, n_lanes), axis=-1, keepdims=True)
    e = jnp.exp(v2 - v1)
    w1 = 1.0 / (1.0 + e)
    w2 = e / (1.0 + e)
    out = jnp.where(lane == ROUTE_FIRST, i1, 0.0)
    out = jnp.where(lane == ROUTE_SECOND, i2, out)
    out = jnp.where(lane == ROUTE_W_FIRST, w1, out)
    return jnp.where(lane == ROUTE_W_SECOND, w2, out)


def _out_proj_kernel(*refs, with_router):
    if with_router:
        m_ref, wo_ref, x_ref, mod_ref, g_ref, wr_ref, br_ref, xo_ref, ho_ref, route_ref = refs
    else:
        m_ref, wo_ref, x_ref, mod_ref, g_ref, xo_ref, ho_ref = refs
    mod = mod_ref[...]
    chunk = m_ref.shape[0] // 2
    for rows in (pl.ds(0, chunk), pl.ds(chunk, chunk)):
        xn = x_ref[rows, :] + mod[2:3] * _dot(m_ref[rows, :], wo_ref[...])
        xo_ref[rows, :] = xn
        h2 = _rms_mod(xn, g_ref[...], mod[3:4], mod[4:5])
        ho_ref[rows, :] = h2.astype(ho_ref.dtype)
        if with_router:
            route_ref[rows, :] = _top2_route(_dot(h2.astype(BF16), wr_ref[...]) + br_ref[...])


def _out_proj(merged, w_o, x, mod, norm_g, layer, seq, router=None, tm=512):
    n, d = x.shape
    tiles_per_seq = seq // tm
    in_specs = [
        pl.BlockSpec((tm, d), lambda i: (i, 0)),
        pl.BlockSpec((d, d), lambda i: (0, 0)),
        pl.BlockSpec((tm, d), lambda i: (i, 0)),
        pl.BlockSpec((None, None, N_MOD, d), lambda i: (layer, i // tiles_per_seq, 0, 0)),
        pl.BlockSpec((None, 1, d), lambda i: (layer, 0, 0)),
    ]
    out_specs = [pl.BlockSpec((tm, d), lambda i: (i, 0)), pl.BlockSpec((tm, d), lambda i: (i, 0))]
    out_shape = [jax.ShapeDtypeStruct((n, d), F32),
                 jax.ShapeDtypeStruct((n, d), BF16 if router is None else F32)]
    args = [merged, w_o, x, mod, norm_g]
    if router is not None:
        w_r, b_r, j = router
        in_specs += [pl.BlockSpec((None, d, V7X_LANES), lambda i: (j, 0, 0)),
                     pl.BlockSpec((None, 1, V7X_LANES), lambda i: (j, 0, 0))]
        out_specs.append(pl.BlockSpec((tm, V7X_LANES), lambda i: (i, 0)))
        out_shape.append(jax.ShapeDtypeStruct((n, V7X_LANES), F32))
        args += [w_r, b_r]
    return pl.pallas_call(
        functools.partial(_out_proj_kernel, with_router=router is not None),
        grid=(n // tm,),
        in_specs=in_specs, out_specs=out_specs, out_shape=out_shape,
        input_output_aliases={2: 0} if layer > 0 else {},
        compiler_params=_params(("arbitrary",), 56),
        name="out_proj",
    )(*args)


def _residual_norm(x, gate, f, g, nmod, last):
    xn = x + gate * f
    if last:
        zero = jnp.zeros_like(g)
        return xn, _rms_mod(xn, g, zero, zero)
    return xn, _rms_mod(xn, g, nmod[0:1], nmod[1:2])


def _swiglu(h, wg_ref, wu_ref, wd_ref):
    a = _dot(h, wg_ref[...])
    act = (a * jax.nn.sigmoid(a)) * _dot(h, wu_ref[...])
    return _dot(act.astype(BF16), wd_ref[...])


def _ffn_kernel(h_ref, wg_ref, wu_ref, wd_ref, x_ref, mod_ref, g_ref, nmod_ref, xo_ref, ho_ref, *, last):
    j = pl.program_id(1)
    n_j = pl.num_programs(1)

    @pl.when(j == 0)
    def _():
        xo_ref[...] = _swiglu(h_ref[...], wg_ref, wu_ref, wd_ref)

    @pl.when(jnp.logical_and(j > 0, j < n_j - 1))
    def _():
        xo_ref[...] += _swiglu(h_ref[...], wg_ref, wu_ref, wd_ref)

    @pl.when(j == n_j - 1)
    def _():
        f = xo_ref[...] + _swiglu(h_ref[...], wg_ref, wu_ref, wd_ref)
        xn, hn = _residual_norm(x_ref[...], mod_ref[...][5:6], f, g_ref[...], nmod_ref[...], last)
        xo_ref[...] = xn
        ho_ref[...] = hn.astype(ho_ref.dtype)


def _ffn(h, wg, wu, wd, w_index, x, mod, layer, next_g, next_g_index, next_layer, seq,
         last=False, tm=512, tf=1024):
    n, d = x.shape
    f = wg.shape[-1]
    tiles_per_seq = seq // tm
    return pl.pallas_call(
        functools.partial(_ffn_kernel, last=last),
        grid=(n // tm, f // tf),
        in_specs=[
            pl.BlockSpec((tm, d), lambda i, j: (i, 0)),
            pl.BlockSpec((None, d, tf), lambda i, j: (w_index, 0, j)),
            pl.BlockSpec((None, d, tf), lambda i, j: (w_index, 0, j)),
            pl.BlockSpec((None, tf, d), lambda i, j: (w_index, j, 0)),
            pl.BlockSpec((tm, d), lambda i, j: (i, 0)),
            pl.BlockSpec((None, None, N_MOD, d), lambda i, j: (layer, i // tiles_per_seq, 0, 0)),
            pl.BlockSpec((None, 1, d), lambda i, j: (next_g_index, 0, 0)),
            pl.BlockSpec((None, None, N_MOD, d), lambda i, j: (next_layer, i // tiles_per_seq, 0, 0)),
        ],
        out_specs=[pl.BlockSpec((tm, d), lambda i, j: (i, 0)),
                   pl.BlockSpec((tm, d), lambda i, j: (i, 0))],
        out_shape=[jax.ShapeDtypeStruct((n, d), F32),
                   jax.ShapeDtypeStruct((n, d), F32 if last else BF16)],
        input_output_aliases={4: 0},
        compiler_params=_params(("arbitrary", "arbitrary"), 58),
        name="ffn",
    )(h, wg, wu, wd, x, mod, next_g, mod)


TOP_K = 2
ROW_DMA_UNROLL = 8


def _route_plan(route, n_experts, tm):
    n = route.shape[0]
    n_tiles = TOP_K * n // tm + n_experts - 1
    choice = route[:, ROUTE_FIRST:ROUTE_SECOND + 1].astype(jnp.int32)
    member = jnp.sum(choice[:, :, None] == jnp.arange(n_experts)[None, None, :], axis=1,
                     dtype=jnp.int32)
    rank = jnp.cumsum(member, axis=0) - member
    counts = rank[-1] + member[-1]
    padded = (counts + tm - 1) // tm * tm
    ends = jnp.cumsum(padded)
    dest = jnp.take_along_axis((ends - padded)[None, :] + rank, choice, axis=1)
    n_used = ends[-1] // tm
    tile = jnp.minimum(jnp.arange(n_tiles), n_used - 1)
    tile_expert = jnp.sum(tile[:, None] * tm >= ends[None, :], axis=1, dtype=jnp.int32)
    tile_rows = jnp.clip(jnp.take(ends - padded + counts, tile_expert) - tile * tm, 0, tm)
    n_rows = n_tiles * tm
    gap_len = jnp.concatenate([padded - counts, n_rows - ends[-1:]])
    gap_start = jnp.concatenate([ends - (padded - counts), ends[-1:]])
    gap_cum = jnp.cumsum(gap_len)
    k = jnp.arange(n_rows - TOP_K * n)
    gap = jnp.sum(k[:, None] >= gap_cum[None, :], axis=1)
    free_rows = jnp.take(gap_start - (gap_cum - gap_len), gap) + k
    return (dest.reshape(-1).astype(jnp.int32), free_rows.astype(jnp.int32), tile_expert,
            tile_rows.astype(jnp.int32), n_used.reshape(1).astype(jnp.int32), n_tiles)


def _dispatch_kernel(dest_ref, free_ref, h_ref, o_ref, zrow, sem, *, free_per_step):
    tt = h_ref.shape[0]
    step = pl.program_id(0)
    base = step * tt

    @pl.when(step == 0)
    def _():
        zrow[...] = jnp.zeros_like(zrow)

    def row_copy(r, k):
        row = dest_ref[TOP_K * (base + r) + k]
        return pltpu.make_async_copy(h_ref.at[pl.ds(r, 1), :], o_ref.at[pl.ds(row, 1), :], sem)

    def zero_copy(q):
        row = free_ref[step * free_per_step + q]
        return pltpu.make_async_copy(zrow.at[pl.ds(0, 1), :], o_ref.at[pl.ds(row, 1), :], sem)

    @pl.loop(0, tt, unroll=ROW_DMA_UNROLL)
    def _(r):
        for k in range(TOP_K):
            row_copy(r, k).start(priority=k)

    @pl.loop(0, free_per_step, unroll=ROW_DMA_UNROLL)
    def _(q):
        zero_copy(q).start()

    for _ in range(TOP_K):
        pltpu.make_async_copy(h_ref, o_ref.at[pl.ds(0, tt), :], sem).wait()
    pltpu.make_async_copy(h_ref.at[pl.ds(0, free_per_step), :],
                          o_ref.at[pl.ds(0, free_per_step), :], sem).wait()


def _dispatch(h, dest, free_rows, n_rows, tt=512):
    n, d = h.shape
    steps = n // tt
    assert free_rows.shape[0] == n_rows - TOP_K * n and free_rows.shape[0] % (steps * ROW_DMA_UNROLL) == 0
    return pl.pallas_call(
        functools.partial(_dispatch_kernel, free_per_step=free_rows.shape[0] // steps),
        grid_spec=pltpu.PrefetchScalarGridSpec(
            num_scalar_prefetch=2,
            grid=(steps,),
            in_specs=[pl.BlockSpec((tt, d), lambda i, dest, free: (i, 0))],
            out_specs=pl.BlockSpec(memory_space=pl.ANY),
            scratch_shapes=[pltpu.VMEM((V7X_SUBLANES, d), h.dtype), pltpu.SemaphoreType.DMA(())],
        ),
        out_shape=jax.ShapeDtypeStruct((n_rows, d), h.dtype),
        compiler_params=_params(("arbitrary",), 32),
        name="dispatch",
    )(dest, free_rows, h)


def _expert_kernel(te_ref, tr_ref, nu_ref, x_ref, wg_ref, wu_ref, wd_ref, y_ref, xb):
    del te_ref
    i = pl.program_id(0)
    j = pl.program_id(1)
    used = i < nu_ref[0]
    half = y_ref.shape[0] // 2
    partial = tr_ref[i] <= half

    @pl.when(jnp.logical_and(jnp.logical_not(used), j == 0))
    def _():
        y_ref[...] = jnp.zeros_like(y_ref)

    def step(rows, first):
        if first:
            h = x_ref[rows, :].astype(BF16)
            xb[rows, :] = h
            y_ref[rows, :] = _swiglu(h, wg_ref, wu_ref, wd_ref)
        else:
            y_ref[rows, :] += _swiglu(xb[rows, :], wg_ref, wu_ref, wd_ref)

    whole, head, tail = slice(None), slice(0, half), slice(half, 2 * half)
    for first in (True, False):
        at_step = (j == 0) if first else (j > 0)

        @pl.when(jnp.logical_and(jnp.logical_and(used, at_step), jnp.logical_not(partial)))
        def _():
            step(whole, first)

        @pl.when(jnp.logical_and(jnp.logical_and(used, at_step), partial))
        def _():
            step(head, first)
            if first:
                y_ref[tail, :] = jnp.zeros((half, y_ref.shape[1]), y_ref.dtype)


def _expert_ffn(xs, wg, wu, wd, w_base, tile_expert, tile_rows, n_used, n_tiles, tm, tf=1024):
    d = xs.shape[-1]
    f = wg.shape[-1]
    n_j = f // tf

    def row(i, j, te, tr, nu):
        return (jnp.minimum(i, nu[0] - 1), 0)

    def col(i, j, nu):
        return jnp.where(i < nu[0], j, n_j - 1)

    return pl.pallas_call(
        _expert_kernel,
        grid_spec=pltpu.PrefetchScalarGridSpec(
            num_scalar_prefetch=3,
            grid=(n_tiles, n_j),
            in_specs=[
                pl.BlockSpec((tm, d), row),
                pl.BlockSpec((None, d, tf), lambda i, j, te, tr, nu: (w_base + te[i], 0, col(i, j, nu))),
                pl.BlockSpec((None, d, tf), lambda i, j, te, tr, nu: (w_base + te[i], 0, col(i, j, nu))),
                pl.BlockSpec((None, tf, d), lambda i, j, te, tr, nu: (w_base + te[i], col(i, j, nu), 0)),
            ],
            out_specs=pl.BlockSpec((tm, d), lambda i, j, te, tr, nu: (i, 0)),
            scratch_shapes=[pltpu.VMEM((tm, d), BF16)],
        ),
        out_shape=jax.ShapeDtypeStruct((n_tiles * tm, d), F32),
        compiler_params=_params(("arbitrary", "arbitrary"), 56),
        name="expert_ffn",
    )(tile_expert, tile_rows, n_used, xs, wg, wu, wd)


def _combine_kernel(dest_ref, y_ref, x_ref, route_ref, mod_ref, g_ref, nmod_ref, *rest, last):
    if last:
        ho_ref, ybuf, sem = rest
    else:
        xo_ref, ho_ref, ybuf, sem = rest
    tc = x_ref.shape[0]
    step = pl.program_id(0)
    slot = step % 2

    def request_rows(s, into):
        @pl.loop(0, tc, unroll=ROW_DMA_UNROLL)
        def _(r):
            for k in range(TOP_K):
                row = dest_ref[TOP_K * (s * tc + r) + k]
                pltpu.make_async_copy(y_ref.at[pl.ds(row, 1), :], ybuf.at[into, k, pl.ds(r, 1), :],
                                      sem.at[into, k]).start(priority=k)

    @pl.when(step == 0)
    def _():
        request_rows(0, 0)

    @pl.when(step + 1 < pl.num_programs(0))
    def _():
        request_rows(step + 1, 1 - slot)

    for k in range(TOP_K):
        pltpu.make_async_copy(y_ref.at[pl.ds(0, tc), :], ybuf.at[slot, k], sem.at[slot, k]).wait()

    route = route_ref[...]
    f = (route[:, ROUTE_W_FIRST:ROUTE_W_FIRST + 1] * ybuf[slot, 0]
         + route[:, ROUTE_W_SECOND:ROUTE_W_SECOND + 1] * ybuf[slot, 1])
    xn, hn = _residual_norm(x_ref[...], mod_ref[...][5:6], f, g_ref[...], nmod_ref[...], last)
    if not last:
        xo_ref[...] = xn
    ho_ref[...] = hn.astype(ho_ref.dtype)


def _combine(y, dest, x, route, mod, layer, next_g, next_g_index, next_layer, seq, last, tc=512):
    n, d = x.shape
    tiles_per_seq = seq // tc
    tok = pl.BlockSpec((tc, d), lambda i, dest: (i, 0))
    out_specs = [tok] if last else [tok, tok]
    out_shape = ([jax.ShapeDtypeStruct((n, d), F32)] if last else
                 [jax.ShapeDtypeStruct((n, d), F32), jax.ShapeDtypeStruct((n, d), BF16)])
    return pl.pallas_call(
        functools.partial(_combine_kernel, last=last),
        grid_spec=pltpu.PrefetchScalarGridSpec(
            num_scalar_prefetch=1,
            grid=(n // tc,),
            in_specs=[
                pl.BlockSpec(memory_space=pl.ANY),
                tok,
                pl.BlockSpec((tc, V7X_LANES), lambda i, dest: (i, 0)),
                pl.BlockSpec((None, None, N_MOD, d), lambda i, dest: (layer, i // tiles_per_seq, 0, 0)),
                pl.BlockSpec((None, 1, d), lambda i, dest: (next_g_index, 0, 0)),
                pl.BlockSpec((None, None, N_MOD, d), lambda i, dest: (next_layer, i // tiles_per_seq, 0, 0)),
            ],
            out_specs=out_specs,
            scratch_shapes=[pltpu.VMEM((2, TOP_K, tc, d), F32), pltpu.SemaphoreType.DMA((2, TOP_K))],
        ),
        out_shape=out_shape,
        input_output_aliases={} if last else {2: 0},
        compiler_params=_params(("arbitrary",), 48),
        name="combine",
    )(dest, y, x, route, mod, next_g, mod)


def _moe(h2, route, wg, wu, wd, w_base, n_experts, x, mod, layer, next_g, next_g_index, next_layer,
         seq, last, tm=512):
    dest, free_rows, tile_expert, tile_rows, n_used, n_tiles = _route_plan(route, n_experts, tm)
    xs = _dispatch(h2, dest, free_rows, n_tiles * tm)
    y = _expert_ffn(xs, wg, wu, wd, w_base, tile_expert, tile_rows, n_used, n_tiles, tm)
    return _combine(y, dest, x, route, mod, layer, next_g, next_g_index, next_layer, seq, last)


def _block_diag_groups(w_heads):
    depth, heads, hd, _ = w_heads.shape
    per = V7X_MXU_DIM // hd
    groups = heads // per
    wg = w_heads.reshape(depth, groups, per, hd, hd)
    eye = jnp.eye(per, dtype=w_heads.dtype)
    out = jnp.einsum("lgpde,pq->lgpdqe", wg, eye)
    return out.reshape(depth, groups, V7X_MXU_DIM, V7X_MXU_DIM)


def kernel(x, c, w_mod, b_mod, norm1_g, norm2_g, w_in, conv_a_w, conv_a_b, w_rg_a, b_rg_a, w_rg_x, b_rg_x, rg_lambda, w_out_a, conv_b_w, conv_b_b, w_out_b, w_o, w_ff_gate, w_ff_up, w_ff_down, w_router, b_router, w_e_gate, w_e_up, w_e_down, final_g):
    bsz, seq, d = x.shape
    depth = w_in.shape[0]
    n = bsz * seq
    n_moe, n_experts = w_router.shape[0], w_router.shape[-1]
    w_rnn = conv_a_w.shape[-1]

    mixer_p = dict(
        conv_a_w=conv_a_w, conv_a_b=conv_a_b.reshape(depth, 1, w_rnn),
        w_rg_a=_block_diag_groups(w_rg_a).astype(BF16), b_rg_a=b_rg_a.reshape(depth, 1, w_rnn),
        w_rg_x=_block_diag_groups(w_rg_x).astype(BF16), b_rg_x=b_rg_x.reshape(depth, 1, w_rnn),
        rg_lambda=rg_lambda.reshape(depth, 1, w_rnn),
        conv_b_w=conv_b_w, conv_b_b=conv_b_b.reshape(depth, 1, w_rnn),
    )
    f_d, f_e = w_ff_gate.shape[-1], w_e_gate.shape[-1]
    dense_w = (w_ff_gate, w_ff_up, w_ff_down)
    expert_w = (w_e_gate.reshape(n_moe, n_experts * d, f_e), w_e_up.reshape(n_moe, n_experts * d, f_e),
                w_e_down.reshape(n_moe, n_experts * f_e, d))
    w_r = jnp.pad(w_router, ((0, 0), (0, 0), (0, V7X_LANES - n_experts))).astype(BF16)
    b_r = jnp.pad(b_router, ((0, 0), (0, V7X_LANES - n_experts)),
                  constant_values=ROUTER_PAD_BIAS).reshape(n_moe, 1, V7X_LANES)
    norm1 = norm1_g.reshape(depth, 1, d)
    norm2 = norm2_g.reshape(depth, 1, d)
    final = final_g.reshape(1, 1, d)

    mod = _modulation(c, w_mod, b_mod).reshape(depth, bsz, N_MOD, d)
    xs = x.reshape(n, d)
    h = _first_norm(xs, mod, norm1, seq)
    for l in range(depth):
        is_last = l == depth - 1
        nxt = (final, 0, l) if is_last else (norm1, l + 1, l + 1)
        j = l // 2
        layer_w = [(w_out_a, l), (w_out_b, l), (w_o, l)]
        if l % 2 == 0:
            z, woa, wob, wo, wg, wu, wd = _in_proj(h, w_in, l, layer_w + [(w, j) for w in dense_w])
            merged = _mixer(z, mixer_p, woa, wob, l, bsz, seq)
            xs, h2 = _out_proj(merged, wo, xs, mod, norm2, l, seq)
            xs, h = _ffn(h2, wg.reshape(1, d, f_d), wu.reshape(1, d, f_d), wd.reshape(1, f_d, d), 0,
                         xs, mod, l, *nxt, seq, last=is_last)
        else:
            z, woa, wob, wo, wg, wu, wd = _in_proj(h, w_in, l, layer_w + [(w, j) for w in expert_w])
            merged = _mixer(z, mixer_p, woa, wob, l, bsz, seq)
            xs, h2, route = _out_proj(merged, wo, xs, mod, norm2, l, seq, router=(w_r, b_r, j))
            outs = _moe(h2, route, wg.reshape(n_experts, d, f_e), wu.reshape(n_experts, d, f_e),
                        wd.reshape(n_experts, f_e, d), 0, n_experts, xs, mod, l, *nxt, seq,
                        is_last)
            if is_last:
                (h,) = outs
            else:
                xs, h = outs
    return h.reshape(bsz, seq, d)
```
